```python
import jax, jax.numpy as jnp
from jax import lax
import numpy as np

D_MODEL = 4096
BATCH = 2
SEQ = 8192
DEPTH = 2
DEC_BATCH = 2
DEC_SEQ = 4096
PAST_LEN = 128

N_HEADS = 32
N_KV_HEADS = 8
HEAD_DIM = D_MODEL // N_HEADS
Q_WIDTH = N_HEADS * HEAD_DIM
KV_WIDTH = N_KV_HEADS * HEAD_DIM
WINDOW = 128
BLOCK = 128
F_WIDTH = D_MODEL // 2
F_GROUPS = 16
F_GROUP_DIM = F_WIDTH // F_GROUPS
N_BRANCHES = 2
IN_WIDTH = F_WIDTH + Q_WIDTH + 2 * KV_WIDTH + N_BRANCHES * D_MODEL
D_FF = 2 * D_MODEL
D_FF_EXPERT = D_MODEL // 4
N_EXPERTS = 8
TOP_K = 2
N_DENSE = (DEPTH + 1) // 2
N_MOE = DEPTH // 2
N_MOD = 6
EPS = 1e-6

kernel_name = "hybrid_fourier_swa_gqa_encoder"


def _rms_norm(x, g):
    xf = x.astype(jnp.float32)
    y = xf * lax.rsqrt(jnp.mean(xf * xf, axis=-1, keepdims=True) + EPS)
    return (y * g.astype(jnp.float32)).astype(x.dtype)


def _modulate(h, shift, scale):
    return h * (1 + scale[:, None, :]) + shift[:, None, :]


def _fourier_mix(u):
    b, s, _ = u.shape
    uf = u.astype(jnp.float32).reshape(b, s, F_GROUPS, F_GROUP_DIM)
    y = jnp.fft.fft2(uf, axes=(1, 3), norm="ortho").real
    return y.reshape(b, s, F_WIDTH).astype(u.dtype)


def _alibi_slopes():
    return jnp.exp2(-8.0 * jnp.arange(1, N_HEADS + 1, dtype=jnp.float32) / N_HEADS)


def _windowed_gqa(q, k, v, sink):
    b, s, _, _ = q.shape
    nb = s // BLOCK
    g = N_HEADS // N_KV_HEADS
    qb = q.reshape(b, nb, BLOCK, N_KV_HEADS, g, HEAD_DIM).transpose(1, 0, 2, 3, 4, 5)

    def band(t):
        tp = jnp.pad(t, ((0, 0), (BLOCK, BLOCK), (0, 0), (0, 0)))
        tp = tp.reshape(b, nb + 2, BLOCK, N_KV_HEADS, HEAD_DIM)
        tw = jnp.concatenate([tp[:, :-2], tp[:, 1:-1], tp[:, 2:]], axis=2)
        return tw.transpose(1, 0, 2, 3, 4)

    kw, vw = band(k), band(v)
    key_pos = jnp.arange(nb)[:, None] * BLOCK - BLOCK + jnp.arange(3 * BLOCK)[None, :]
    key_valid = (key_pos >= 0) & (key_pos < s)
    rel = jnp.arange(BLOCK)[:, None] + BLOCK - jnp.arange(3 * BLOCK)[None, :]
    dist = jnp.abs(rel).astype(jnp.float32)
    in_band = jnp.abs(rel) <= WINDOW
    alibi = (-_alibi_slopes()[:, None, None] * dist[None]).reshape(N_KV_HEADS, g, BLOCK, 3 * BLOCK)
    sink_l = sink.astype(jnp.float32).reshape(1, N_KV_HEADS, g, 1, 1)
    scale = HEAD_DIM ** -0.5

    def one_block(args):
        qi, ki, vi, vali = args
        sc = jnp.einsum('bqkgd,bskd->bkgqs', qi.astype(jnp.float32), ki.astype(jnp.float32)) * scale
        sc = sc + alibi[None]
        mask = in_band[None, None, None] & vali[None, None, None, None, :]
        sc = jnp.where(mask, sc, -jnp.inf)
        m = jnp.maximum(jnp.max(sc, axis=-1, keepdims=True), sink_l)
        p = jnp.exp(sc - m)
        denom = jnp.sum(p, axis=-1, keepdims=True) + jnp.exp(sink_l - m)
        o = jnp.einsum('bkgqs,bskd->bqkgd', p / denom, vi.astype(jnp.float32))
        return o.astype(qi.dtype)

    out = lax.map(one_block, (qb, kw, vw, key_valid))
    return out.transpose(1, 0, 2, 3, 4, 5).reshape(b, s, Q_WIDTH)


def _swiglu(h, w_gate, w_up, w_down):
    return jnp.einsum('bsf,fd->bsd', jax.nn.silu(jnp.einsum('bsd,df->bsf', h, w_gate)) * jnp.einsum('bsd,df->bsf', h, w_up), w_down)


def _moe_swiglu(h, w_router, b_router, w_gate, w_up, w_down):
    logits = jnp.einsum('bsd,de->bse', h.astype(jnp.float32), w_router.astype(jnp.float32)) + b_router.astype(jnp.float32)
    top_v, top_i = lax.top_k(logits, TOP_K)
    top_w = jax.nn.softmax(top_v, axis=-1)
    combine = jnp.sum(jax.nn.one_hot(top_i, N_EXPERTS, dtype=jnp.float32) * top_w[..., None], axis=-2)
    combine = combine.astype(h.dtype)
    y = jnp.zeros_like(h)
    for e in range(N_EXPERTS):
        y = y + combine[..., e:e + 1] * _swiglu(h, w_gate[e], w_up[e], w_down[e])
    return y


def _trunk(x, c, w_ada, b_ada, g_pre_mix, g_post_mix, w_in, w_fourier_out, w_attn_out, attn_sink, w_out,
           g_pre_ffn, g_post_ffn, w_dense_gate, w_dense_up, w_dense_down, w_router, b_router,
           w_moe_gate, w_moe_up, w_moe_down):
    b, s, _ = x.shape
    for l in range(DEPTH):
        mod = (jnp.einsum('bd,dm->bm', jax.nn.silu(c.astype(jnp.float32)), w_ada[l].astype(jnp.float32)) + b_ada[l].astype(jnp.float32)).astype(x.dtype)
        sh1, sc1, ga1, sh2, sc2, ga2 = jnp.split(mod, N_MOD, axis=-1)
        h = _modulate(_rms_norm(x, g_pre_mix[l]), sh1, sc1)
        p = jnp.einsum('bsd,dn->bsn', h, w_in[l])
        o0 = F_WIDTH
        o1 = o0 + Q_WIDTH
        o2 = o1 + KV_WIDTH
        o3 = o2 + KV_WIDTH
        u = p[..., :o0]
        q = p[..., o0:o1].reshape(b, s, N_HEADS, HEAD_DIM)
        k = p[..., o1:o2].reshape(b, s, N_KV_HEADS, HEAD_DIM)
        v = p[..., o2:o3].reshape(b, s, N_KV_HEADS, HEAD_DIM)
        gates = jax.nn.sigmoid(p[..., o3:].astype(jnp.float32)).astype(x.dtype)
        gate_f, gate_a = gates[..., :D_MODEL], gates[..., D_MODEL:]
        f_br = jnp.einsum('bsf,fd->bsd', _fourier_mix(u), w_fourier_out[l])
        a_br = jnp.einsum('bsq,qd->bsd', _windowed_gqa(q, k, v, attn_sink[l]), w_attn_out[l])
        m = jnp.einsum('bsd,de->bse', gate_f * f_br + gate_a * a_br, w_out[l])
        x = x + ga1[:, None, :] * _rms_norm(m, g_post_mix[l])
        h = _modulate(_rms_norm(x, g_pre_ffn[l]), sh2, sc2)
        if l % 2 == 0:
            i = l // 2
            y = _swiglu(h, w_dense_gate[i], w_dense_up[i], w_dense_down[i])
        else:
            i = l // 2
            y = _moe_swiglu(h, w_router[i], b_router[i], w_moe_gate[i], w_moe_up[i], w_moe_down[i])
        x = x + ga2[:, None, :] * _rms_norm(y, g_post_ffn[l])
    return x


def setup_inputs(seed: int = 0) -> dict:
    key = jax.random.key(seed)
    ks = jax.random.split(key, 26)

    def nrm(k, shape, scale):
        return jax.random.normal(k, shape, jnp.float32) * scale

    return {
        "x_prompt": nrm(ks[0], (BATCH, SEQ, D_MODEL), 1.0),
        "x_sample": nrm(ks[1], (DEC_BATCH, DEC_SEQ, D_MODEL), 1.0),
        "c_prompt": nrm(ks[2], (BATCH, D_MODEL), 1.0),
        "c_sample": nrm(ks[3], (DEC_BATCH, D_MODEL), 1.0),
        "w_ada": nrm(ks[4], (DEPTH, D_MODEL, N_MOD * D_MODEL), 0.5 * D_MODEL ** -0.5),
        "b_ada": nrm(ks[5], (DEPTH, N_MOD * D_MODEL), 0.01),
        "g_pre_mix": 1.0 + nrm(ks[6], (DEPTH, D_MODEL), 0.05),
        "g_post_mix": 1.0 + nrm(ks[7], (DEPTH, D_MODEL), 0.05),
        "w_in": nrm(ks[8], (DEPTH, D_MODEL, IN_WIDTH), D_MODEL ** -0.5),
        "w_fourier_out": nrm(ks[9], (DEPTH, F_WIDTH, D_MODEL), F_WIDTH ** -0.5),
        "w_attn_out": nrm(ks[10], (DEPTH, Q_WIDTH, D_MODEL), Q_WIDTH ** -0.5),
        "attn_sink": nrm(ks[11], (DEPTH, N_HEADS), 1.0),
        "w_out": nrm(ks[12], (DEPTH, D_MODEL, D_MODEL), D_MODEL ** -0.5),
        "g_pre_ffn": 1.0 + nrm(ks[13], (DEPTH, D_MODEL), 0.05),
        "g_post_ffn": 1.0 + nrm(ks[14], (DEPTH, D_MODEL), 0.05),
        "w_dense_gate": nrm(ks[15], (N_DENSE, D_MODEL, D_FF), D_MODEL ** -0.5),
        "w_dense_up": nrm(ks[16], (N_DENSE, D_MODEL, D_FF), D_MODEL ** -0.5),
        "w_dense_down": nrm(ks[17], (N_DENSE, D_FF, D_MODEL), D_FF ** -0.5),
        "w_router": nrm(ks[18], (N_MOE, D_MODEL, N_EXPERTS), D_MODEL ** -0.5),
        "b_router": nrm(ks[19], (N_MOE, N_EXPERTS), 0.01),
        "w_moe_gate": nrm(ks[20], (N_MOE, N_EXPERTS, D_MODEL, D_FF_EXPERT), D_MODEL ** -0.5),
        "w_moe_up": nrm(ks[21], (N_MOE, N_EXPERTS, D_MODEL, D_FF_EXPERT), D_MODEL ** -0.5),
        "w_moe_down": nrm(ks[22], (N_MOE, N_EXPERTS, D_FF_EXPERT, D_MODEL), D_FF_EXPERT ** -0.5),
    }


def reference(x_prompt, x_sample, c_prompt, c_sample, w_ada, b_ada, g_pre_mix, g_post_mix, w_in, w_fourier_out,
              w_attn_out, attn_sink, w_out, g_pre_ffn, g_post_ffn, w_dense_gate, w_dense_up, w_dense_down,
              w_router, b_router, w_moe_gate, w_moe_up, w_moe_down):
    y_prompt = _trunk(x_prompt, c_prompt, w_ada, b_ada, g_pre_mix, g_post_mix, w_in, w_fourier_out, w_attn_out,
                      attn_sink, w_out, g_pre_ffn, g_post_ffn, w_dense_gate, w_dense_up, w_dense_down,
                      w_router, b_router, w_moe_gate, w_moe_up, w_moe_down)
    y_sample = _trunk(x_sample, c_sample, w_ada, b_ada, g_pre_mix, g_post_mix, w_in, w_fourier_out, w_attn_out,
                      attn_sink, w_out, g_pre_ffn, g_post_ffn, w_dense_gate, w_dense_up, w_dense_down,
                      w_router, b_router, w_moe_gate, w_moe_up, w_moe_down)
    return (y_prompt, y_sample)
```

```python
import functools

import numpy as np
import jax
import jax.numpy as jnp
from jax import lax
from jax.experimental import pallas as pl
from jax.experimental.pallas import tpu as pltpu

N_HEADS = 32
N_KV_HEADS = 8
WINDOW = 128
F_GROUPS = 16
N_EXPERTS = 8
N_MOD = 6
EPS = 1e-6

LANES = 128
BF16_SUBLANES = 16
VMEM_LIMIT_BYTES = 56 * 1024 * 1024

F32 = jnp.float32
BF16 = jnp.bfloat16
NEG_BIG = -1e30


def _params(*semantics):
    return pltpu.CompilerParams(dimension_semantics=semantics, vmem_limit_bytes=VMEM_LIMIT_BYTES)


def _tile(n, pref):
    t = min(n, pref)
    while n % t:
        t -= 1
    return t


def _adaln_kernel(c_ref, w_ref, b_ref, o_ref):
    c = c_ref[...]
    s = (c * jax.nn.sigmoid(c)).astype(BF16)
    o_ref[...] = jnp.dot(s, w_ref[...].astype(BF16), preferred_element_type=F32) + b_ref[...]


def _adaln(c_all, w_ada, b_ada):
    depth, d, n = w_ada.shape
    r = c_all.shape[0]
    tn = _tile(n, 512)
    return pl.pallas_call(
        _adaln_kernel,
        grid=(depth, n // tn),
        in_specs=[
            pl.BlockSpec((r, d), lambda l, j: (0, 0)),
            pl.BlockSpec((None, d, tn), lambda l, j: (l, 0, j)),
            pl.BlockSpec((None, 1, tn), lambda l, j: (l, 0, j)),
        ],
        out_specs=pl.BlockSpec((None, r, tn), lambda l, j: (l, 0, j)),
        out_shape=jax.ShapeDtypeStruct((depth, r, n), F32),
        compiler_params=_params("parallel", "parallel"),
        name="adaln",
    )(c_all, w_ada, b_ada.reshape(depth, 1, n))


def _rms(x, g):
    return x * lax.rsqrt(jnp.mean(x * x, axis=-1, keepdims=True) + EPS) * g


def _prenorm_kernel(x_ref, g_ref, sh_ref, sc_ref, h_ref):
    h = _rms(x_ref[...], g_ref[...]) * (1.0 + sc_ref[...]) + sh_ref[...]
    h_ref[...] = h.astype(h_ref.dtype)


def _row_specs(ts, d):
    tok = pl.BlockSpec((None, ts, d), lambda b, i: (b, i, 0))
    vec = pl.BlockSpec((1, d), lambda b, i: (0, 0))
    per_seq = pl.BlockSpec((None, 1, d), lambda b, i: (b, 0, 0))
    return tok, vec, per_seq


def _prenorm(x, g, shift, scale):
    b, s, d = x.shape
    ts = _tile(s, 256)
    tok, vec, per_seq = _row_specs(ts, d)
    return pl.pallas_call(
        _prenorm_kernel,
        grid=(b, s // ts),
        in_specs=[tok, vec, per_seq, per_seq],
        out_specs=tok,
        out_shape=jax.ShapeDtypeStruct((b, s, d), BF16),
        compiler_params=_params("parallel", "parallel"),
        name="prenorm",
    )(x, g.reshape(1, d), shift, scale)


def _post_kernel(x_ref, m_ref, gp_ref, ga_ref, *rest, with_next):
    xn = x_ref[...] + ga_ref[...] * _rms(m_ref[...].astype(F32), gp_ref[...])
    if with_next:
        gn_ref, sh_ref, sc_ref, xo_ref, h_ref = rest
        xo_ref[...] = xn
        h = _rms(xn, gn_ref[...]) * (1.0 + sc_ref[...]) + sh_ref[...]
        h_ref[...] = h.astype(h_ref.dtype)
    else:
        (xo_ref,) = rest
        xo_ref[...] = xn


def _post(x, m, g_post, gate, nxt=None):
    b, s, d = x.shape
    ts = _tile(s, 256)
    tok, vec, per_seq = _row_specs(ts, d)
    ins = [x, m.reshape(b, s, d), g_post.reshape(1, d), gate]
    in_specs = [tok, tok, vec, per_seq]
    out_shape = [jax.ShapeDtypeStruct((b, s, d), F32)]
    out_specs = [tok]
    if nxt is not None:
        g_next, shift, scale = nxt
        ins += [g_next.reshape(1, d), shift, scale]
        in_specs += [vec, per_seq, per_seq]
        out_shape.append(jax.ShapeDtypeStruct((b, s, d), BF16))
        out_specs.append(tok)
    out = pl.pallas_call(
        functools.partial(_post_kernel, with_next=nxt is not None),
        grid=(b, s // ts),
        in_specs=in_specs,
        out_specs=out_specs,
        out_shape=out_shape,
        compiler_params=_params("parallel", "parallel"),
        name="post",
    )(*ins)
    return out if nxt is not None else out[0]


def _mm_kernel(a_ref, w_ref, o_ref, *scratch, nk, sigmoid_from):
    part = jnp.dot(a_ref[...], w_ref[...], preferred_element_type=F32)

    def finish(acc):
        if sigmoid_from is None:
            o_ref[...] = acc.astype(o_ref.dtype)
        else:
            j = pl.program_id(1)

            @pl.when(j < sigmoid_from)
            def _():
                o_ref[...] = acc.astype(o_ref.dtype)

            @pl.when(j >= sigmoid_from)
            def _():
                o_ref[...] = jax.nn.sigmoid(acc).astype(o_ref.dtype)

    if nk == 1:
        finish(part)
        return
    (acc_ref,) = scratch
    k = pl.program_id(2)

    @pl.when(k == 0)
    def _():
        acc_ref[...] = part

    @pl.when(jnp.logical_and(k > 0, k < nk - 1))
    def _():
        acc_ref[...] += part

    @pl.when(k == nk - 1)
    def _():
        finish(acc_ref[...] + part)


def _mm(a, w, *, sigmoid_from_col=None, tm=1024, tn=1024, tk=4096):
    m, kdim = a.shape
    n = w.shape[1]
    tm, tn, tk = _tile(m, tm), _tile(n, tn), _tile(kdim, tk)
    nk = kdim // tk
    assert nk >= 1 and (nk == 1 or nk >= 2)
    sig = None
    if sigmoid_from_col is not None:
        assert sigmoid_from_col % tn == 0
        sig = sigmoid_from_col // tn
    return pl.pallas_call(
        functools.partial(_mm_kernel, nk=nk, sigmoid_from=sig),
        grid=(m // tm, n // tn, nk),
        in_specs=[
            pl.BlockSpec((tm, tk), lambda i, j, k: (i, k)),
            pl.BlockSpec((tk, tn), lambda i, j, k: (k, j)),
        ],
        out_specs=pl.BlockSpec((tm, tn), lambda i, j, k: (i, j)),
        out_shape=jax.ShapeDtypeStruct((m, n), BF16),
        scratch_shapes=[pltpu.VMEM((tm, tn), F32)] if nk > 1 else [],
        compiler_params=_params("parallel", "parallel", "arbitrary"),
        name="mm",
    )(a, w)


def _merge_kernel(y_ref, a_ref, wf_ref, wa_ref, gf_ref, ga_ref, o_ref):
    f = jnp.dot(y_ref[...], wf_ref[...], preferred_element_type=F32)
    a = jnp.dot(a_ref[...], wa_ref[...], preferred_element_type=F32)
    o = gf_ref[...].astype(F32) * f + ga_ref[...].astype(F32) * a
    o_ref[...] = o.astype(o_ref.dtype)


def _merge(y, a, p, w_f, w_a, gate_col0):
    m, f = y.shape
    q = a.shape[1]
    d = w_f.shape[1]
    tm, tn = _tile(m, 1024), _tile(d, 512)
    assert gate_col0 % tn == 0
    g0 = gate_col0 // tn
    nd = d // tn
    return pl.pallas_call(
        _merge_kernel,
        grid=(m // tm, nd),
        in_specs=[
            pl.BlockSpec((tm, f), lambda i, j: (i, 0)),
            pl.BlockSpec((tm, q), lambda i, j: (i, 0)),
            pl.BlockSpec((f, tn), lambda i, j: (0, j)),
            pl.BlockSpec((q, tn), lambda i, j: (0, j)),
            pl.BlockSpec((tm, tn), lambda i, j: (i, g0 + j)),
            pl.BlockSpec((tm, tn), lambda i, j: (i, g0 + nd + j)),
        ],
        out_specs=pl.BlockSpec((tm, tn), lambda i, j: (i, j)),
        out_shape=jax.ShapeDtypeStruct((m, d), BF16),
        compiler_params=_params("parallel", "parallel"),
        name="merge",
    )(y, a, w_f, w_a, p, p)


def _glu_kernel(h_ref, wg_ref, wu_ref, *rest, tiles_per_expert):
    h = h_ref[...]
    g = jnp.dot(h, wg_ref[...], preferred_element_type=F32)
    u = jnp.dot(h, wu_ref[...], preferred_element_type=F32)
    act = g * jax.nn.sigmoid(g) * u
    if tiles_per_expert is None:
        (o_ref,) = rest
    else:
        comb_ref, o_ref = rest
        e = pl.program_id(1) // tiles_per_expert
        comb = comb_ref[...]
        lane = lax.broadcasted_iota(jnp.int32, comb.shape, 1)
        act = act * jnp.sum(jnp.where(lane == e, comb, 0.0), axis=-1, keepdims=True)
    o_ref[...] = act.astype(o_ref.dtype)


def _glu(h, w_gate, w_up, combine=None):
    m, d = h.shape
    ne, _, fe = w_gate.shape
    tm, tn = _tile(m, 1024), _tile(fe, 512)
    tpe = fe // tn
    wspec = pl.BlockSpec((None, d, tn), lambda i, j: (j // tpe, 0, j % tpe))
    ins = [h, w_gate, w_up]
    in_specs = [pl.BlockSpec((tm, d), lambda i, j: (i, 0)), wspec, wspec]
    if combine is not None:
        ins.append(combine)
        in_specs.append(pl.BlockSpec((tm, ne), lambda i, j: (i, 0)))
    return pl.pallas_call(
        functools.partial(_glu_kernel, tiles_per_expert=tpe if combine is not None else None),
        grid=(m // tm, ne * tpe),
        in_specs=in_specs,
        out_specs=pl.BlockSpec((tm, tn), lambda i, j: (i, j)),
        out_shape=jax.ShapeDtypeStruct((m, ne * fe), BF16),
        compiler_params=_params("parallel", "parallel"),
        name="glu",
    )(*ins)


def _router_kernel(x_ref, g_ref, sh_ref, sc_ref, w_ref, b_ref, o_ref):
    h = _rms(x_ref[...], g_ref[...]) * (1.0 + sc_ref[...]) + sh_ref[...]
    logits = jnp.dot(h, w_ref[...], preferred_element_type=F32, precision=lax.Precision.HIGHEST) + b_ref[...]
    ne = logits.shape[-1]
    lane = lax.broadcasted_iota(jnp.int32, logits.shape, 1)
    v1 = jnp.max(logits, axis=-1, keepdims=True)
    i1 = jnp.min(jnp.where(logits == v1, lane, ne), axis=-1, keepdims=True)
    rest = jnp.where(lane == i1, -jnp.inf, logits)
    v2 = jnp.max(rest, axis=-1, keepdims=True)
    i2 = jnp.min(jnp.where(rest == v2, lane, ne), axis=-1, keepdims=True)
    e2 = jnp.exp(v2 - v1)
    w1 = 1.0 / (1.0 + e2)
    w2 = e2 / (1.0 + e2)
    o_ref[...] = jnp.where(lane == i1, w1, 0.0) + jnp.where(lane == i2, w2, 0.0)


def _router(x, g, shift, scale, w_router, b_router):
    b, s, d = x.shape
    ne = w_router.shape[1]
    ts = _tile(s, 256)
    tok, vec, per_seq = _row_specs(ts, d)
    return pl.pallas_call(
        _router_kernel,
        grid=(b, s // ts),
        in_specs=[tok, vec, per_seq, per_seq,
                  pl.BlockSpec((d, ne), lambda bi, i: (0, 0)),
                  pl.BlockSpec((1, ne), lambda bi, i: (0, 0))],
        out_specs=pl.BlockSpec((None, ts, ne), lambda bi, i: (bi, i, 0)),
        out_shape=jax.ShapeDtypeStruct((b, s, ne), F32),
        compiler_params=_params("parallel", "parallel"),
        name="router",
    )(x, g.reshape(1, d), shift, scale, w_router, b_router.reshape(1, ne))


def _attn_kernel(q_ref, kp_ref, km_ref, kn_ref, vp_ref, vm_ref, vn_ref, nd_ref, hc_ref, o_ref,
                 k_scr, v_scr, *, tq, seq, group, hd, scale):
    w = WINDOW
    i = pl.program_id(2)
    k_scr[0:w] = kp_ref[...]
    k_scr[w:w + tq] = km_ref[...]
    k_scr[w + tq:] = kn_ref[...]
    v_scr[0:w] = vp_ref[...]
    v_scr[w:w + tq] = vm_ref[...]
    v_scr[w + tq:] = vn_ref[...]
    negdist = nd_ref[...]
    for s in range(tq // w):
        kw = k_scr[s * w:(s + 3) * w]
        vw = v_scr[s * w:(s + 3) * w]
        key_pos = lax.broadcasted_iota(jnp.int32, (1, 3 * w), 1) + (i * tq + (s - 1) * w)
        valid = jnp.logical_and(key_pos >= 0, key_pos < seq)
        for h in range(group):
            hc = hc_ref[h]
            sink = hc[0:1, 0:1]
            slope = hc[1:2, 0:1]
            q = q_ref[s * w:(s + 1) * w, h * hd:(h + 1) * hd]
            sc = lax.dot_general(q, kw, (((1,), (1,)), ((), ())), preferred_element_type=F32)
            sc = sc * scale + slope * negdist
            sc = jnp.where(valid, sc, NEG_BIG)
            mx = jnp.maximum(jnp.max(sc, axis=-1, keepdims=True), sink)
            p = jnp.exp(sc - mx)
            denom = jnp.sum(p, axis=-1, keepdims=True) + jnp.exp(sink - mx)
            o = jnp.dot(p.astype(BF16), vw, preferred_element_type=F32) / denom
            o_ref[s * w:(s + 1) * w, h * hd:(h + 1) * hd] = o.astype(o_ref.dtype)


def _attention(p, head_consts, q_col0, k_col0, v_col0):
    b, s, _ = p.shape
    w = WINDOW
    group = N_HEADS // N_KV_HEADS
    hd = LANES
    gw = group * hd
    tq = _tile(s, 512)
    assert s % w == 0 and tq % w == 0 and q_col0 % gw == 0 and k_col0 % hd == 0 and v_col0 % hd == 0
    r = tq // w
    nblk = s // w
    qb, kb, vb = q_col0 // gw, k_col0 // hd, v_col0 // hd

    def halo_specs(c0):
        return [
            pl.BlockSpec((None, w, hd), lambda bi, g, i: (bi, jnp.maximum(i * r - 1, 0), c0 + g)),
            pl.BlockSpec((None, tq, hd), lambda bi, g, i: (bi, i, c0 + g)),
            pl.BlockSpec((None, w, hd), lambda bi, g, i: (bi, jnp.minimum((i + 1) * r, nblk - 1), c0 + g)),
        ]

    rel = np.arange(w)[:, None] + w - np.arange(3 * w)[None, :]
    negdist = np.where(np.abs(rel) <= WINDOW, -np.abs(rel), NEG_BIG).astype(np.float32)
    return pl.pallas_call(
        functools.partial(_attn_kernel, tq=tq, seq=s, group=group, hd=hd, scale=float(hd) ** -0.5),
        grid=(b, N_KV_HEADS, s // tq),
        in_specs=[pl.BlockSpec((None, tq, gw), lambda bi, g, i: (bi, i, qb + g))]
        + halo_specs(kb) + halo_specs(vb)
        + [pl.BlockSpec((w, 3 * w), lambda bi, g, i: (0, 0)),
           pl.BlockSpec((group, 2, LANES), lambda bi, g, i: (g, 0, 0))],
        out_specs=pl.BlockSpec((None, tq, gw), lambda bi, g, i: (bi, i, g)),
        out_shape=jax.ShapeDtypeStruct((b, s, N_HEADS * hd), BF16),
        scratch_shapes=[pltpu.VMEM((tq + 2 * w, hd), BF16), pltpu.VMEM((tq + 2 * w, hd), BF16)],
        compiler_params=_params("parallel", "parallel", "parallel"),
        name="attention",
    )(p, p, p, p, p, p, p, jnp.asarray(negdist), head_consts)


R_FFT = BF16_SUBLANES


def _fft_factors(s):
    n2 = min(256, s // R_FFT)
    n1 = s // n2
    assert n1 * n2 == s and n1 % R_FFT == 0 and n2 % R_FFT == 0
    return n1, n2


def _fft_tables(s, gdim):
    n1, n2 = _fft_factors(s)
    r = R_FFT
    k1 = np.arange(n1).reshape(1, n1, 1, 1, 1, 1)
    jj = np.arange(r).reshape(1, 1, 1, r, 1, 1)
    nn1 = np.arange(n1).reshape(1, 1, 1, 1, n1, 1)
    jc = np.arange(r).reshape(1, 1, 1, 1, 1, r)
    jb = np.arange(n2 // r).reshape(n2 // r, 1, 1, 1, 1, 1)
    ang = 2.0 * np.pi * ((nn1 * k1 * n2 + (jb * r + jj) * k1) % s) / s
    ang = np.broadcast_to(ang, (n2 // r, n1, 1, r, n1, 1))
    eye = (jj == jc)
    t1 = np.concatenate([np.cos(ang) * eye, np.sin(ang) * eye], axis=2)
    t1 = t1.reshape(n2 // r, n1 * 2 * r, n1 * r)
    k2 = np.arange(n2).reshape(n2, 1, 1)
    nn2 = (np.arange(n2 // r).reshape(1, n2 // r, 1) * r + np.arange(r).reshape(1, 1, r))
    beta = 2.0 * np.pi * ((k2 * nn2) % n2) / n2
    c2, s2 = np.cos(beta), np.sin(beta)
    row_a = np.stack([c2, -s2], axis=2)
    row_b = np.stack([s2, c2], axis=2)
    f2 = np.concatenate([row_a, row_b], axis=0).reshape(2 * n2, 2 * n2) / np.sqrt(s)
    cc = np.arange(gdim)
    phi = 2.0 * np.pi * ((cc[:, None] * cc[None, :]) % gdim) / gdim
    wc = np.concatenate([np.cos(phi), -np.sin(phi)], axis=0) / np.sqrt(gdim)
    return (jnp.asarray(t1.astype(np.float32)).astype(BF16),
            jnp.asarray(f2.astype(np.float32)).astype(BF16),
            jnp.asarray(wc.astype(np.float32)).astype(BF16))


def _fft1_kernel(t_ref, u_ref, z_ref):
    n1, r, c = u_ref.shape
    u = u_ref[...].reshape(n1 * r, c)
    z_ref[...] = jnp.dot(t_ref[...], u, preferred_element_type=F32).astype(z_ref.dtype)


def _fft2_kernel(f2_ref, wc_ref, z_ref, y_ref, scr, *, gdim):
    nj, rows, ct = z_ref.shape
    per_k1 = rows // R_FFT
    n2 = nj * R_FFT
    for kk in range(R_FFT):
        zin = z_ref[:, kk * per_k1:(kk + 1) * per_k1, :].reshape(2 * n2, ct)
        res = jnp.dot(f2_ref[...], zin, preferred_element_type=F32)
        a = res[:n2].astype(BF16)
        bm = res[n2:].astype(BF16)
        ys = []
        for g in range(ct // gdim):
            lhs = jnp.concatenate([a[:, g * gdim:(g + 1) * gdim], bm[:, g * gdim:(g + 1) * gdim]], axis=1)
            ys.append(jnp.dot(lhs, wc_ref[...], preferred_element_type=F32))
        scr[:, kk, :] = jnp.concatenate(ys, axis=1)
    y_ref[...] = scr[...].astype(y_ref.dtype)


def _fourier(p, fwidth):
    b, s, width = p.shape
    gdim = fwidth // F_GROUPS
    assert gdim == LANES
    n1, n2 = _fft_factors(s)
    r = R_FFT
    t1, f2, wc = _fft_tables(s, gdim)
    nj = n2 // r
    z = pl.pallas_call(
        _fft1_kernel,
        grid=(nj, b),
        in_specs=[
            pl.BlockSpec((None, 2 * n1 * r, n1 * r), lambda j, bi: (j, 0, 0)),
            pl.BlockSpec((None, n1, r, fwidth), lambda j, bi: (bi, 0, j, 0)),
        ],
        out_specs=pl.BlockSpec((None, None, 2 * n1 * r, fwidth), lambda j, bi: (bi, j, 0, 0)),
        out_shape=jax.ShapeDtypeStruct((b, nj, 2 * n1 * r, fwidth), BF16),
        compiler_params=_params("parallel", "parallel"),
        name="fft_stage1",
    )(t1, p.reshape(b, n1, n2, width))
    ct = _tile(fwidth, 256)
    nkb = n1 // r
    z = z.reshape(b, nj, nkb, 2 * r * r, fwidth)
    y = pl.pallas_call(
        functools.partial(_fft2_kernel, gdim=gdim),
        grid=(b, nkb, fwidth // ct),
        in_specs=[
            pl.BlockSpec((2 * n2, 2 * n2), lambda bi, kb, c: (0, 0)),
            pl.BlockSpec((2 * gdim, gdim), lambda bi, kb, c: (0, 0)),
            pl.BlockSpec((None, nj, None, 2 * r * r, ct), lambda bi, kb, c: (bi, 0, kb, 0, c)),
        ],
        out_specs=pl.BlockSpec((None, n2, None, r, ct), lambda bi, kb, c: (bi, 0, kb, 0, c)),
        out_shape=jax.ShapeDtypeStruct((b, n2, nkb, r, fwidth), BF16),
        scratch_shapes=[pltpu.VMEM((n2, r, ct), F32)],
        compiler_params=_params("parallel", "parallel", "parallel"),
        name="fft_stage2",
    )(f2, wc, z)
    return y.reshape(b, s, fwidth)


def _trunk(x, mod, wts):
    b, s, d = x.shape
    m = b * s
    depth = wts["w_in"].shape[0]
    fwidth = d // 2
    qwidth = d
    kvwidth = N_KV_HEADS * (d // N_HEADS)
    q0 = fwidth
    k0 = q0 + qwidth
    v0 = k0 + kvwidth
    g0 = v0 + kvwidth

    def mods(l):
        return [mod[l, :, k * d:(k + 1) * d].reshape(b, 1, d) for k in range(N_MOD)]

    sh1, sc1, ga1, sh2, sc2, ga2 = mods(0)
    h = _prenorm(x, wts["g_pre_mix"][0], sh1, sc1)
    for l in range(depth):
        p = _mm(h.reshape(m, d), wts["w_in"][l], sigmoid_from_col=g0)
        p3 = p.reshape(b, s, -1)
        y = _fourier(p3, fwidth)
        a = _attention(p3, wts["head_consts"][l], q0, k0, v0)
        mg = _merge(y.reshape(m, fwidth), a.reshape(m, qwidth), p, wts["w_fourier_out"][l],
                    wts["w_attn_out"][l], g0)
        mo = _mm(mg, wts["w_out"][l])
        x, h2 = _post(x, mo, wts["g_post_mix"][l], ga1, (wts["g_pre_ffn"][l], sh2, sc2))
        i = l // 2
        if l % 2 == 0:
            hid = _glu(h2.reshape(m, d), wts["w_dense_gate"][i][None], wts["w_dense_up"][i][None])
            w_down = wts["w_dense_down"][i]
        else:
            comb = _router(x, wts["g_pre_ffn"][l], sh2, sc2, wts["w_router"][i], wts["b_router"][i])
            hid = _glu(h2.reshape(m, d), wts["w_moe_gate"][i], wts["w_moe_up"][i], comb.reshape(m, -1))
            w_down = wts["w_moe_down"][i].reshape(-1, d)
        yo = _mm(hid, w_down)
        if l + 1 < depth:
            gate2 = ga2
            sh1, sc1, ga1, sh2, sc2, ga2 = mods(l + 1)
            x, h = _post(x, yo, wts["g_post_ffn"][l], gate2, (wts["g_pre_mix"][l + 1], sh1, sc1))
        else:
            x = _post(x, yo, wts["g_post_ffn"][l], ga2)
    return x


def kernel(x_prompt, x_sample, c_prompt, c_sample, w_ada, b_ada, g_pre_mix, g_post_mix, w_in, w_fourier_out,
           w_attn_out, attn_sink, w_out, g_pre_ffn, g_post_ffn, w_dense_gate, w_dense_up, w_dense_down,
           w_router, b_router, w_moe_gate, w_moe_up, w_moe_down):
    bp, bs = c_prompt.shape[0], c_sample.shape[0]
    d = c_prompt.shape[1]
    rows = -(-(bp + bs) // 8) * 8
    c_all = jnp.zeros((rows, d), F32).at[:bp].set(c_prompt).at[bp:bp + bs].set(c_sample)
    mod = _adaln(c_all, w_ada, b_ada)

    slopes = jnp.exp2(-8.0 * jnp.arange(1, N_HEADS + 1, dtype=F32) / N_HEADS)
    head_consts = jnp.stack([attn_sink.astype(F32), jnp.broadcast_to(slopes, attn_sink.shape)], axis=-1)
    head_consts = jnp.broadcast_to(head_consts[..., None], head_consts.shape + (LANES,))

    wts = dict(
        w_in=w_in.astype(BF16), w_fourier_out=w_fourier_out.astype(BF16), w_attn_out=w_attn_out.astype(BF16),
        w_out=w_out.astype(BF16), w_dense_gate=w_dense_gate.astype(BF16), w_dense_up=w_dense_up.astype(BF16),
        w_dense_down=w_dense_down.astype(BF16), w_moe_gate=w_moe_gate.astype(BF16),
        w_moe_up=w_moe_up.astype(BF16), w_moe_down=w_moe_down.astype(BF16),
        w_router=w_router, b_router=b_router, head_consts=head_consts,
        g_pre_mix=g_pre_mix, g_post_mix=g_post_mix, g_pre_ffn=g_pre_ffn, g_post_ffn=g_post_ffn,
    )
    y_prompt = _trunk(x_prompt, mod[:, :bp], wts)
    y_sample = _trunk(x_sample, mod[:, bp:bp + bs], wts)
    return (y_prompt, y_sample)
```

```python
import functools

import numpy as np
import jax
import jax.numpy as jnp
from jax import lax
from jax.experimental import pallas as pl
from jax.experimental.pallas import tpu as pltpu

N_HEADS = 32
N_KV_HEADS = 8
WINDOW = 128
F_GROUPS = 16
N_EXPERTS = 8
N_MOD = 6
EPS = 1e-6

LANES = 128
BF16_SUBLANES = 16
VMEM_LIMIT_BYTES = 56 * 1024 * 1024

F32 = jnp.float32
BF16 = jnp.bfloat16
NEG_BIG = -1e30


def _params(*semantics):
    return pltpu.CompilerParams(dimension_semantics=semantics, vmem_limit_bytes=VMEM_LIMIT_BYTES)


def _tile(n, pref):
    t = min(n, pref)
    while n % t:
        t -= 1
    return t


def _adaln_kernel(c_ref, w_ref, b_ref, o_ref):
    c = c_ref[...]
    s = (c * jax.nn.sigmoid(c)).astype(BF16)
    o_ref[...] = jnp.dot(s, w_ref[...].astype(BF16), preferred_element_type=F32) + b_ref[...]


def _adaln(c_all, w_ada, b_ada):
    depth, d, n = w_ada.shape
    r = c_all.shape[0]
    tn = _tile(n, 512)
    return pl.pallas_call(
        _adaln_kernel,
        grid=(depth, n // tn),
        in_specs=[
            pl.BlockSpec((r, d), lambda l, j: (0, 0)),
            pl.BlockSpec((None, d, tn), lambda l, j: (l, 0, j)),
            pl.BlockSpec((None, 1, tn), lambda l, j: (l, 0, j)),
        ],
        out_specs=pl.BlockSpec((None, r, tn), lambda l, j: (l, 0, j)),
        out_shape=jax.ShapeDtypeStruct((depth, r, n), F32),
        compiler_params=_params("parallel", "parallel"),
        name="adaln",
    )(c_all, w_ada, b_ada.reshape(depth, 1, n))


def _rms(x, g):
    return x * lax.rsqrt(jnp.mean(x * x, axis=-1, keepdims=True) + EPS) * g


def _prenorm_kernel(x_ref, g_ref, sh_ref, sc_ref, h_ref):
    h = _rms(x_ref[...], g_ref[...]) * (1.0 + sc_ref[...]) + sh_ref[...]
    h_ref[...] = h.astype(h_ref.dtype)


def _row_specs(ts, d):
    tok = pl.BlockSpec((None, ts, d), lambda b, i: (b, i, 0))
    vec = pl.BlockSpec((1, d), lambda b, i: (0, 0))
    per_seq = pl.BlockSpec((None, 1, d), lambda b, i: (b, 0, 0))
    return tok, vec, per_seq


def _prenorm(x, g, shift, scale):
    b, s, d = x.shape
    ts = _tile(s, 256)
    tok, vec, per_seq = _row_specs(ts, d)
    return pl.pallas_call(
        _prenorm_kernel,
        grid=(b, s // ts),
        in_specs=[tok, vec, per_seq, per_seq],
        out_specs=tok,
        out_shape=jax.ShapeDtypeStruct((b, s, d), BF16),
        compiler_params=_params("parallel", "parallel"),
        name="prenorm",
    )(x, g.reshape(1, d), shift, scale)


def _post_kernel(x_ref, m_ref, gp_ref, ga_ref, *rest, with_next):
    xn = x_ref[...] + ga_ref[...] * _rms(m_ref[...].astype(F32), gp_ref[...])
    if with_next:
        gn_ref, sh_ref, sc_ref, xo_ref, h_ref = rest
        xo_ref[...] = xn
        h = _rms(xn, gn_ref[...]) * (1.0 + sc_ref[...]) + sh_ref[...]
        h_ref[...] = h.astype(h_ref.dtype)
    else:
        (xo_ref,) = rest
        xo_ref[...] = xn


def _post(x, m, g_post, gate, nxt=None, h_dtype=BF16):
    b, s, d = x.shape
    ts = _tile(s, 256)
    tok, vec, per_seq = _row_specs(ts, d)
    ins = [x, m.reshape(b, s, d), g_post.reshape(1, d), gate]
    in_specs = [tok, tok, vec, per_seq]
    out_shape = [jax.ShapeDtypeStruct((b, s, d), F32)]
    out_specs = [tok]
    if nxt is not None:
        g_next, shift, scale = nxt
        ins += [g_next.reshape(1, d), shift, scale]
        in_specs += [vec, per_seq, per_seq]
        out_shape.append(jax.ShapeDtypeStruct((b, s, d), h_dtype))
        out_specs.append(tok)
    out = pl.pallas_call(
        functools.partial(_post_kernel, with_next=nxt is not None),
        grid=(b, s // ts),
        in_specs=in_specs,
        out_specs=out_specs,
        out_shape=out_shape,
        compiler_params=_params("parallel", "parallel"),
        name="post",
    )(*ins)
    return out if nxt is not None else out[0]


def _mm_kernel(a_ref, w_ref, o_ref, *scratch, nk, sigmoid_from):
    part = jnp.dot(a_ref[...], w_ref[...], preferred_element_type=F32)

    def finish(acc):
        if sigmoid_from is None:
            o_ref[...] = acc.astype(o_ref.dtype)
        else:
            j = pl.program_id(1)

            @pl.when(j < sigmoid_from)
            def _():
                o_ref[...] = acc.astype(o_ref.dtype)

            @pl.when(j >= sigmoid_from)
            def _():
                o_ref[...] = jax.nn.sigmoid(acc).astype(o_ref.dtype)

    if nk == 1:
        finish(part)
        return
    (acc_ref,) = scratch
    k = pl.program_id(2)

    @pl.when(k == 0)
    def _():
        acc_ref[...] = part

    @pl.when(jnp.logical_and(k > 0, k < nk - 1))
    def _():
        acc_ref[...] += part

    @pl.when(k == nk - 1)
    def _():
        finish(acc_ref[...] + part)


def _mm(a, w, *, sigmoid_from_col=None, tm=1024, tn=1024, tk=4096):
    m, kdim = a.shape
    n = w.shape[1]
    tm, tn, tk = _tile(m, tm), _tile(n, tn), _tile(kdim, tk)
    nk = kdim // tk
    assert nk >= 1 and (nk == 1 or nk >= 2)
    sig = None
    if sigmoid_from_col is not None:
        assert sigmoid_from_col % tn == 0
        sig = sigmoid_from_col // tn
    return pl.pallas_call(
        functools.partial(_mm_kernel, nk=nk, sigmoid_from=sig),
        grid=(m // tm, n // tn, nk),
        in_specs=[
            pl.BlockSpec((tm, tk), lambda i, j, k: (i, k)),
            pl.BlockSpec((tk, tn), lambda i, j, k: (k, j)),
        ],
        out_specs=pl.BlockSpec((tm, tn), lambda i, j, k: (i, j)),
        out_shape=jax.ShapeDtypeStruct((m, n), BF16),
        scratch_shapes=[pltpu.VMEM((tm, tn), F32)] if nk > 1 else [],
        compiler_params=_params("parallel", "parallel", "arbitrary"),
        name="mm",
    )(a, w)


def _merge_kernel(y_ref, a_ref, wf_ref, wa_ref, gf_ref, ga_ref, o_ref):
    f = jnp.dot(y_ref[...], wf_ref[...], preferred_element_type=F32)
    a = jnp.dot(a_ref[...], wa_ref[...], preferred_element_type=F32)
    o = gf_ref[...].astype(F32) * f + ga_ref[...].astype(F32) * a
    o_ref[...] = o.astype(o_ref.dtype)


def _merge(y, a, p, w_f, w_a, gate_col0):
    m, f = y.shape
    q = a.shape[1]
    d = w_f.shape[1]
    tm, tn = _tile(m, 1024), _tile(d, 512)
    assert gate_col0 % tn == 0
    g0 = gate_col0 // tn
    nd = d // tn
    return pl.pallas_call(
        _merge_kernel,
        grid=(m // tm, nd),
        in_specs=[
            pl.BlockSpec((tm, f), lambda i, j: (i, 0)),
            pl.BlockSpec((tm, q), lambda i, j: (i, 0)),
            pl.BlockSpec((f, tn), lambda i, j: (0, j)),
            pl.BlockSpec((q, tn), lambda i, j: (0, j)),
            pl.BlockSpec((tm, tn), lambda i, j: (i, g0 + j)),
            pl.BlockSpec((tm, tn), lambda i, j: (i, g0 + nd + j)),
        ],
        out_specs=pl.BlockSpec((tm, tn), lambda i, j: (i, j)),
        out_shape=jax.ShapeDtypeStruct((m, d), BF16),
        compiler_params=_params("parallel", "parallel"),
        name="merge",
    )(y, a, w_f, w_a, p, p)


def _glu_kernel(h_ref, wg_ref, wu_ref, o_ref):
    h = h_ref[...]
    g = jnp.dot(h, wg_ref[...], preferred_element_type=F32)
    u = jnp.dot(h, wu_ref[...], preferred_element_type=F32)
    o_ref[...] = (g * jax.nn.sigmoid(g) * u).astype(o_ref.dtype)


def _glu(h, w_gate, w_up):
    m, d = h.shape
    f = w_gate.shape[1]
    tm, tn = _tile(m, 1024), _tile(f, 512)
    wspec = pl.BlockSpec((d, tn), lambda i, j: (0, j))
    return pl.pallas_call(
        _glu_kernel,
        grid=(m // tm, f // tn),
        in_specs=[pl.BlockSpec((tm, d), lambda i, j: (i, 0)), wspec, wspec],
        out_specs=pl.BlockSpec((tm, tn), lambda i, j: (i, j)),
        out_shape=jax.ShapeDtypeStruct((m, f), BF16),
        compiler_params=_params("parallel", "parallel"),
        name="glu",
    )(h, w_gate, w_up)


ROUTE_LANES = 8
EXPERT_TILE = 256


def _router_kernel(h_ref, w_ref, b_ref, o_ref):
    logits = jnp.dot(h_ref[...], w_ref[...], preferred_element_type=F32,
                     precision=lax.Precision.HIGHEST) + b_ref[...]
    ne = logits.shape[-1]
    lane = lax.broadcasted_iota(jnp.int32, logits.shape, 1)
    v1 = jnp.max(logits, axis=-1, keepdims=True)
    i1 = jnp.min(jnp.where(logits == v1, lane, ne), axis=-1, keepdims=True)
    rest = jnp.where(lane == i1, -jnp.inf, logits)
    v2 = jnp.max(rest, axis=-1, keepdims=True)
    i2 = jnp.min(jnp.where(rest == v2, lane, ne), axis=-1, keepdims=True)
    e2 = jnp.exp(v2 - v1)
    w1 = 1.0 / (1.0 + e2)
    w2 = e2 / (1.0 + e2)
    out_lane = lax.broadcasted_iota(jnp.int32, o_ref.shape, 1)
    o_ref[...] = jnp.where(out_lane == 0, w1,
                           jnp.where(out_lane == 1, w2,
                                     jnp.where(out_lane == 2, i1.astype(F32),
                                               jnp.where(out_lane == 3, i2.astype(F32), 0.0))))


def _router(h, w_router, b_router):
    m, d = h.shape
    ne = w_router.shape[1]
    ts = _tile(m, 256)
    return pl.pallas_call(
        _router_kernel,
        grid=(m // ts,),
        in_specs=[pl.BlockSpec((ts, d), lambda i: (i, 0)),
                  pl.BlockSpec((d, ne), lambda i: (0, 0)),
                  pl.BlockSpec((1, ne), lambda i: (0, 0))],
        out_specs=pl.BlockSpec((ts, ROUTE_LANES), lambda i: (i, 0)),
        out_shape=jax.ShapeDtypeStruct((m, ROUTE_LANES), F32),
        compiler_params=_params("parallel"),
        name="router",
    )(h, w_router, b_router.reshape(1, ne))


def _rank_kernel(route_ref, tri_ref, rank_ref, cnt_ref, carry, *, ne):
    @pl.when(pl.program_id(0) == 0)
    def _():
        carry[...] = jnp.zeros_like(carry)

    r = route_ref[...]
    lane = lax.broadcasted_iota(jnp.int32, (r.shape[0], 2 * ne), 1)
    sel = jnp.where(lane < ne, r[:, 2:3], r[:, 3:4] + float(ne))
    onehot = lane.astype(F32) == sel
    oh = jnp.where(onehot, 1.0, 0.0)
    before = jnp.dot(tri_ref[...], oh.astype(BF16), preferred_element_type=F32) + carry[...]
    ranks = jnp.where(onehot, before, 0.0)
    r0 = jnp.sum(jnp.where(lane < ne, ranks, 0.0), axis=-1, keepdims=True)
    r1 = jnp.sum(jnp.where(lane >= ne, ranks, 0.0), axis=-1, keepdims=True)
    out_lane = lax.broadcasted_iota(jnp.int32, rank_ref.shape, 1)
    rank_ref[...] = jnp.where(out_lane == 0, r0, jnp.where(out_lane == 1, r1, 0.0))
    carry[...] += jnp.sum(oh, axis=0, keepdims=True)
    cnt_ref[...] = carry[...]


def _rank(route, ne):
    m = route.shape[0]
    tr = _tile(m, 512)
    tri = np.tril(np.ones((tr, tr), np.float32), -1)
    return pl.pallas_call(
        functools.partial(_rank_kernel, ne=ne),
        grid=(m // tr,),
        in_specs=[pl.BlockSpec((tr, ROUTE_LANES), lambda i: (i, 0)),
                  pl.BlockSpec((tr, tr), lambda i: (0, 0))],
        out_specs=[pl.BlockSpec((tr, ROUTE_LANES), lambda i: (i, 0)),
                   pl.BlockSpec((1, 2 * ne), lambda i: (0, 0))],
        out_shape=[jax.ShapeDtypeStruct((m, ROUTE_LANES), F32),
                   jax.ShapeDtypeStruct((1, 2 * ne), F32)],
        scratch_shapes=[pltpu.VMEM((1, 2 * ne), F32)],
        compiler_params=_params("arbitrary"),
        name="rank",
    )(route, jnp.asarray(tri).astype(BF16))


def _moe_plan(route, rank, counts, ne, tm):
    m = route.shape[0]
    nt = (2 * m) // tm + ne
    c0 = counts[0, :ne].astype(jnp.int32)
    c = c0 + counts[0, ne:].astype(jnp.int32)
    padded = ((c + tm - 1) // tm) * tm
    ends = jnp.cumsum(padded)
    off = ends - padded
    i1 = route[:, 2].astype(jnp.int32)
    i2 = route[:, 3].astype(jnp.int32)
    pos0 = off[i1] + rank[:, 0].astype(jnp.int32)
    pos1 = off[i2] + c0[i2] + rank[:, 1].astype(jnp.int32)
    tok = jnp.arange(m, dtype=jnp.int32)
    src = jnp.zeros((nt * tm,), jnp.int32).at[pos0].set(tok).at[pos1].set(tok)
    tile_expert = jnp.sum(jnp.arange(nt, dtype=jnp.int32)[:, None] * tm >= ends[None, :], axis=1)
    tile_expert = jnp.minimum(tile_expert, ne - 1).astype(jnp.int32)
    n_used = (ends[-1] // tm).astype(jnp.int32).reshape(1)
    return dict(src=src, pos=jnp.concatenate([pos0, pos1]), tile_expert=tile_expert, n_used=n_used, nt=nt)


def _glu_sparse_kernel(src_ref, te_ref, nu_ref, h_hbm, wg_ref, wu_ref, o_ref, lhs, sem, *, tm):
    t = pl.program_id(0)
    n_used = nu_ref[0]
    slot = t % 2

    def start_tile(tile, sl):
        def body(r, carry):
            pltpu.make_async_copy(h_hbm.at[pl.ds(src_ref[tile * tm + r], 1)],
                                  lhs.at[sl, pl.ds(r, 1)], sem.at[sl]).start()
            return carry
        lax.fori_loop(0, tm, body, 0, unroll=8)

    @pl.when(t == 0)
    def _():
        start_tile(0, 0)

    @pl.when(t + 1 < n_used)
    def _():
        start_tile(t + 1, 1 - slot)

    @pl.when(t < n_used)
    def _():
        def body(r, carry):
            pltpu.make_async_copy(h_hbm.at[pl.ds(0, 1)], lhs.at[slot, pl.ds(0, 1)], sem.at[slot]).wait()
            return carry
        lax.fori_loop(0, tm, body, 0, unroll=8)
        x = lhs[slot].astype(BF16)
        g = jnp.dot(x, wg_ref[...], preferred_element_type=F32)
        u = jnp.dot(x, wu_ref[...], preferred_element_type=F32)
        o_ref[...] = (g * jax.nn.sigmoid(g) * u).astype(o_ref.dtype)

    @pl.when(t >= n_used)
    def _():
        o_ref[...] = jnp.zeros_like(o_ref)


def _glu_sparse(h, w_gate, w_up, plan, tm):
    m, d = h.shape
    _, _, fe = w_gate.shape
    nt = plan["nt"]
    wspec = pl.BlockSpec((None, d, fe), lambda t, src, te, nu: (te[t], 0, 0))
    return pl.pallas_call(
        functools.partial(_glu_sparse_kernel, tm=tm),
        grid_spec=pltpu.PrefetchScalarGridSpec(
            num_scalar_prefetch=3,
            grid=(nt,),
            in_specs=[pl.BlockSpec(memory_space=pl.ANY), wspec, wspec],
            out_specs=pl.BlockSpec((tm, fe), lambda t, src, te, nu: (t, 0)),
            scratch_shapes=[pltpu.VMEM((2, tm, d), F32), pltpu.SemaphoreType.DMA((2,))],
        ),
        out_shape=jax.ShapeDtypeStruct((nt * tm, fe), BF16),
        compiler_params=_params("arbitrary"),
        name="glu_sparse",
    )(plan["src"], plan["tile_expert"], plan["n_used"], h, w_gate, w_up)


def _down_sparse_kernel(te_ref, nu_ref, hid_ref, w_ref, o_ref):
    t = pl.program_id(0)

    @pl.when(t < nu_ref[0])
    def _():
        o_ref[...] = jnp.dot(hid_ref[...], w_ref[...], preferred_element_type=F32)

    @pl.when(t >= nu_ref[0])
    def _():
        o_ref[...] = jnp.zeros_like(o_ref)


def _down_sparse(hid, w_down, plan, tm):
    _, fe, d = w_down.shape
    nt = plan["nt"]
    return pl.pallas_call(
        _down_sparse_kernel,
        grid_spec=pltpu.PrefetchScalarGridSpec(
            num_scalar_prefetch=2,
            grid=(nt,),
            in_specs=[pl.BlockSpec((tm, fe), lambda t, te, nu: (t, 0)),
                      pl.BlockSpec((None, fe, d), lambda t, te, nu: (te[t], 0, 0))],
            out_specs=pl.BlockSpec((tm, d), lambda t, te, nu: (t, 0)),
        ),
        out_shape=jax.ShapeDtypeStruct((nt * tm, d), F32),
        compiler_params=_params("arbitrary"),
        name="down_sparse",
    )(plan["tile_expert"], plan["n_used"], hid, w_down)


def _post_moe_kernel(pos_ref, x_ref, route_ref, gp_ref, ga_ref, ys_hbm, *rest, ts, m_total, with_next):
    if with_next:
        gn_ref, sh_ref, sc_ref, xo_ref, h_ref, buf, sem = rest
    else:
        xo_ref, buf, sem = rest
    i = pl.program_id(0)
    slot = i % 2

    def start_tile(tile, sl):
        def body(r, carry):
            for k in range(2):
                row = pos_ref[k * m_total + tile * ts + r]
                pltpu.make_async_copy(ys_hbm.at[pl.ds(row, 1)], buf.at[sl, k, pl.ds(r, 1)], sem.at[sl]).start()
            return carry
        lax.fori_loop(0, ts, body, 0, unroll=4)

    @pl.when(i == 0)
    def _():
        start_tile(0, 0)

    @pl.when(i + 1 < pl.num_programs(0))
    def _():
        start_tile(i + 1, 1 - slot)

    def wait_body(r, carry):
        pltpu.make_async_copy(ys_hbm.at[pl.ds(0, 1)], buf.at[slot, 0, pl.ds(0, 1)], sem.at[slot]).wait()
        return carry
    lax.fori_loop(0, 2 * ts, wait_body, 0, unroll=8)

    route = route_ref[...]
    y = route[:, 0:1] * buf[slot, 0] + route[:, 1:2] * buf[slot, 1]
    xn = x_ref[...] + ga_ref[...] * _rms(y, gp_ref[...])
    xo_ref[...] = xn
    if with_next:
        h = _rms(xn, gn_ref[...]) * (1.0 + sc_ref[...]) + sh_ref[...]
        h_ref[...] = h.astype(h_ref.dtype)


def _post_moe(x, ys, route, plan, g_post, gate, nxt=None):
    b, s, d = x.shape
    m = b * s
    ts = _tile(s, 256)
    per_s = s // ts
    tok = pl.BlockSpec((ts, d), lambda i, pos: (i, 0))
    vec = pl.BlockSpec((1, d), lambda i, pos: (0, 0))
    per_seq = pl.BlockSpec((None, 1, d), lambda i, pos: (i // per_s, 0, 0))
    ins = [x.reshape(m, d), route, g_post.reshape(1, d), gate, ys]
    in_specs = [tok, pl.BlockSpec((ts, ROUTE_LANES), lambda i, pos: (i, 0)), vec, per_seq,
                pl.BlockSpec(memory_space=pl.ANY)]
    out_shape = [jax.ShapeDtypeStruct((m, d), F32)]
    out_specs = [tok]
    if nxt is not None:
        g_next, shift, scale = nxt
        ins += [g_next.reshape(1, d), shift, scale]
        in_specs += [vec, per_seq, per_seq]
        out_shape.append(jax.ShapeDtypeStruct((m, d), BF16))
        out_specs.append(tok)
    out = pl.pallas_call(
        functools.partial(_post_moe_kernel, ts=ts, m_total=m, with_next=nxt is not None),
        grid_spec=pltpu.PrefetchScalarGridSpec(
            num_scalar_prefetch=1,
            grid=(m // ts,),
            in_specs=in_specs,
            out_specs=out_specs,
            scratch_shapes=[pltpu.VMEM((2, 2, ts, d), F32), pltpu.SemaphoreType.DMA((2,))],
        ),
        out_shape=out_shape,
        compiler_params=_params("arbitrary"),
        name="post_moe",
    )(plan["pos"], *ins)
    if nxt is not None:
        return out[0].reshape(b, s, d), out[1].reshape(b, s, d)
    return out[0].reshape(b, s, d)


LOG2E = 1.4426950408889634


def _attn_kernel(q_ref, kp_ref, km_ref, kn_ref, vp_ref, vm_ref, vn_ref, bias_ref, sink_ref, o_ref,
                 k_scr, v_scr, *, tq, seq, group, hd, scale):
    w = WINDOW
    i = pl.program_id(2)
    k_scr[0:w] = kp_ref[...]
    k_scr[w:w + tq] = km_ref[...]
    k_scr[w + tq:] = kn_ref[...]
    v_scr[0:w, 0:hd] = vp_ref[...]
    v_scr[w:w + tq, 0:hd] = vm_ref[...]
    v_scr[w + tq:, 0:hd] = vn_ref[...]
    v_scr[:, hd:] = jnp.ones((tq + 2 * w, hd), v_scr.dtype)
    c = scale * LOG2E
    bias = bias_ref[...]
    sink = sink_ref[:, 0:1]
    nsub = tq // w
    for s in range(nsub):
        kw = k_scr[s * w:(s + 3) * w]
        vw = v_scr[s * w:(s + 3) * w]
        q = jnp.concatenate([q_ref[s * w:(s + 1) * w, h * hd:(h + 1) * hd] for h in range(group)], axis=0)
        sc = lax.dot_general(q, kw, (((1,), (1,)), ((), ())), preferred_element_type=F32) + bias
        if s == 0 or s == nsub - 1:
            key_pos = lax.broadcasted_iota(jnp.int32, (1, 3 * w), 1) + (i * tq + (s - 1) * w)
            valid = jnp.logical_and(key_pos >= 0, key_pos < seq)
            sc = sc + jnp.where(valid, 0.0, NEG_BIG)
        mx = jnp.maximum(jnp.max(sc, axis=-1, keepdims=True), sink)
        p = jnp.exp2((sc - mx) * c)
        pv = jnp.dot(p.astype(BF16), vw, preferred_element_type=F32)
        denom = pv[:, hd:hd + 1] + jnp.exp2((sink - mx) * c)
        o = (pv[:, 0:hd] / denom).astype(o_ref.dtype)
        for h in range(group):
            o_ref[s * w:(s + 1) * w, h * hd:(h + 1) * hd] = o[h * w:(h + 1) * w]


def _attention(p, sink, q_col0, k_col0, v_col0):
    b, s, _ = p.shape
    w = WINDOW
    group = N_HEADS // N_KV_HEADS
    hd = LANES
    gw = group * hd
    tq = _tile(s, 512)
    assert s % w == 0 and tq % w == 0 and q_col0 % gw == 0 and k_col0 % hd == 0 and v_col0 % hd == 0
    r = tq // w
    nblk = s // w
    qb, kb, vb = q_col0 // gw, k_col0 // hd, v_col0 // hd
    scale = float(hd) ** -0.5

    def halo_specs(c0):
        return [
            pl.BlockSpec((None, w, hd), lambda bi, g, i: (bi, jnp.maximum(i * r - 1, 0), c0 + g)),
            pl.BlockSpec((None, tq, hd), lambda bi, g, i: (bi, i, c0 + g)),
            pl.BlockSpec((None, w, hd), lambda bi, g, i: (bi, jnp.minimum((i + 1) * r, nblk - 1), c0 + g)),
        ]

    rel = np.abs(np.arange(w)[:, None] + w - np.arange(3 * w)[None, :]).astype(np.float64)
    slopes = np.exp2(-8.0 * np.arange(1, N_HEADS + 1, dtype=np.float64) / N_HEADS)
    bias = np.where(rel[None] <= WINDOW, -slopes[:, None, None] * rel[None] / scale, NEG_BIG)
    bias = bias.reshape(N_HEADS * w, 3 * w).astype(np.float32)
    sink_rows = jnp.broadcast_to((sink.astype(F32) / scale)[:, None, None], (N_HEADS, w, LANES))
    sink_rows = sink_rows.reshape(N_HEADS * w, LANES)
    return pl.pallas_call(
        functools.partial(_attn_kernel, tq=tq, seq=s, group=group, hd=hd, scale=scale),
        grid=(b, N_KV_HEADS, s // tq),
        in_specs=[pl.BlockSpec((None, tq, gw), lambda bi, g, i: (bi, i, qb + g))]
        + halo_specs(kb) + halo_specs(vb)
        + [pl.BlockSpec((group * w, 3 * w), lambda bi, g, i: (g, 0)),
           pl.BlockSpec((group * w, LANES), lambda bi, g, i: (g, 0))],
        out_specs=pl.BlockSpec((None, tq, gw), lambda bi, g, i: (bi, i, g)),
        out_shape=jax.ShapeDtypeStruct((b, s, N_HEADS * hd), BF16),
        scratch_shapes=[pltpu.VMEM((tq + 2 * w, hd), BF16), pltpu.VMEM((tq + 2 * w, 2 * hd), BF16)],
        compiler_params=_params("parallel", "parallel", "parallel"),
        name="attention",
    )(p, p, p, p, p, p, p, jnp.asarray(bias), sink_rows)


R_FFT = BF16_SUBLANES


def _fft_factors(s):
    n2 = min(256, s // R_FFT)
    n1 = s // n2
    assert n1 * n2 == s and n1 % R_FFT == 0 and n2 % R_FFT == 0
    return n1, n2


def _fft_tables(s, gdim):
    n1, n2 = _fft_factors(s)
    r = R_FFT
    k1 = np.arange(n1).reshape(1, n1, 1, 1, 1, 1)
    jj = np.arange(r).reshape(1, 1, 1, r, 1, 1)
    nn1 = np.arange(n1).reshape(1, 1, 1, 1, n1, 1)
    jc = np.arange(r).reshape(1, 1, 1, 1, 1, r)
    jb = np.arange(n2 // r).reshape(n2 // r, 1, 1, 1, 1, 1)
    ang = 2.0 * np.pi * ((nn1 * k1 * n2 + (jb * r + jj) * k1) % s) / s
    ang = np.broadcast_to(ang, (n2 // r, n1, 1, r, n1, 1))
    eye = (jj == jc)
    t1 = np.concatenate([np.cos(ang) * eye, np.sin(ang) * eye], axis=2)
    t1 = t1.reshape(n2 // r, n1 * 2 * r, n1 * r)
    k2 = np.arange(n2).reshape(n2, 1, 1)
    nn2 = (np.arange(n2 // r).reshape(1, n2 // r, 1) * r + np.arange(r).reshape(1, 1, r))
    beta = 2.0 * np.pi * ((k2 * nn2) % n2) / n2
    c2, s2 = np.cos(beta), np.sin(beta)
    row_a = np.stack([c2, -s2], axis=2)
    row_b = np.stack([s2, c2], axis=2)
    f2 = np.concatenate([row_a, row_b], axis=0).reshape(2 * n2, 2 * n2) / np.sqrt(s)
    cc = np.arange(gdim)
    phi = 2.0 * np.pi * ((cc[:, None] * cc[None, :]) % gdim) / gdim
    wc = np.concatenate([np.cos(phi), -np.sin(phi)], axis=0) / np.sqrt(gdim)
    return (jnp.asarray(t1.astype(np.float32)).astype(BF16),
            jnp.asarray(f2.astype(np.float32)).astype(BF16),
            jnp.asarray(wc.astype(np.float32)).astype(BF16))


def _fft1_kernel(t_ref, u_ref, z_ref):
    n1, r, c = u_ref.shape
    u = u_ref[...].reshape(n1 * r, c)
    z_ref[...] = jnp.dot(t_ref[...], u, preferred_element_type=F32).astype(z_ref.dtype)


def _fft2_kernel(f2_ref, wc_ref, z_ref, y_ref, scr, *, gdim):
    nj, rows, ct = z_ref.shape
    per_k1 = rows // R_FFT
    n2 = nj * R_FFT
    for kk in range(R_FFT):
        zin = z_ref[:, kk * per_k1:(kk + 1) * per_k1, :].reshape(2 * n2, ct)
        res = jnp.dot(f2_ref[...], zin, preferred_element_type=F32)
        a = res[:n2].astype(BF16)
        bm = res[n2:].astype(BF16)
        ys = []
        for g in range(ct // gdim):
            lhs = jnp.concatenate([a[:, g * gdim:(g + 1) * gdim], bm[:, g * gdim:(g + 1) * gdim]], axis=1)
            ys.append(jnp.dot(lhs, wc_ref[...], preferred_element_type=F32))
        scr[:, kk, :] = jnp.concatenate(ys, axis=1)
    y_ref[...] = scr[...].astype(y_ref.dtype)


def _fourier(p, fwidth):
    b, s, width = p.shape
    gdim = fwidth // F_GROUPS
    assert gdim == LANES
    n1, n2 = _fft_factors(s)
    r = R_FFT
    t1, f2, wc = _fft_tables(s, gdim)
    nj = n2 // r
    z = pl.pallas_call(
        _fft1_kernel,
        grid=(nj, b),
        in_specs=[
            pl.BlockSpec((None, 2 * n1 * r, n1 * r), lambda j, bi: (j, 0, 0)),
            pl.BlockSpec((None, n1, r, fwidth), lambda j, bi: (bi, 0, j, 0)),
        ],
        out_specs=pl.BlockSpec((None, None, 2 * n1 * r, fwidth), lambda j, bi: (bi, j, 0, 0)),
        out_shape=jax.ShapeDtypeStruct((b, nj, 2 * n1 * r, fwidth), BF16),
        compiler_params=_params("parallel", "parallel"),
        name="fft_stage1",
    )(t1, p.reshape(b, n1, n2, width))
    ct = _tile(fwidth, 256)
    nkb = n1 // r
    z = z.reshape(b, nj, nkb, 2 * r * r, fwidth)
    y = pl.pallas_call(
        functools.partial(_fft2_kernel, gdim=gdim),
        grid=(b, nkb, fwidth // ct),
        in_specs=[
            pl.BlockSpec((2 * n2, 2 * n2), lambda bi, kb, c: (0, 0)),
            pl.BlockSpec((2 * gdim, gdim), lambda bi, kb, c: (0, 0)),
            pl.BlockSpec((None, nj, None, 2 * r * r, ct), lambda bi, kb, c: (bi, 0, kb, 0, c)),
        ],
        out_specs=pl.BlockSpec((None, n2, None, r, ct), lambda bi, kb, c: (bi, 0, kb, 0, c)),
        out_shape=jax.ShapeDtypeStruct((b, n2, nkb, r, fwidth), BF16),
        scratch_shapes=[pltpu.VMEM((n2, r, ct), F32)],
        compiler_params=_params("parallel", "parallel", "parallel"),
        name="fft_stage2",
    )(f2, wc, z)
    return y.reshape(b, s, fwidth)


def _trunk(x, mod, wts):
    b, s, d = x.shape
    m = b * s
    depth = wts["w_in"].shape[0]
    fwidth = d // 2
    qwidth = d
    kvwidth = N_KV_HEADS * (d // N_HEADS)
    q0 = fwidth
    k0 = q0 + qwidth
    v0 = k0 + kvwidth
    g0 = v0 + kvwidth

    def mods(l):
        return [mod[l, :, k * d:(k + 1) * d].reshape(b, 1, d) for k in range(N_MOD)]

    sh1, sc1, ga1, sh2, sc2, ga2 = mods(0)
    h = _prenorm(x, wts["g_pre_mix"][0], sh1, sc1)
    for l in range(depth):
        p = _mm(h.reshape(m, d), wts["w_in"][l], sigmoid_from_col=g0)
        p3 = p.reshape(b, s, -1)
        y = _fourier(p3, fwidth)
        a = _attention(p3, wts["attn_sink"][l], q0, k0, v0)
        mg = _merge(y.reshape(m, fwidth), a.reshape(m, qwidth), p, wts["w_fourier_out"][l],
                    wts["w_attn_out"][l], g0)
        mo = _mm(mg, wts["w_out"][l])
        i = l // 2
        dense = l % 2 == 0
        x, h2 = _post(x, mo, wts["g_post_mix"][l], ga1, (wts["g_pre_ffn"][l], sh2, sc2),
                      h_dtype=BF16 if dense else F32)
        gate2 = ga2
        nxt = None
        if l + 1 < depth:
            sh1, sc1, ga1, sh2, sc2, ga2 = mods(l + 1)
            nxt = (wts["g_pre_mix"][l + 1], sh1, sc1)
        if dense:
            hid = _glu(h2.reshape(m, d), wts["w_dense_gate"][i], wts["w_dense_up"][i])
            out = _post(x, _mm(hid, wts["w_dense_down"][i]), wts["g_post_ffn"][l], gate2, nxt)
        else:
            ne = wts["w_router"].shape[-1]
            h2 = h2.reshape(m, d)
            route = _router(h2, wts["w_router"][i], wts["b_router"][i])
            rank, counts = _rank(route, ne)
            plan = _moe_plan(route, rank, counts, ne, EXPERT_TILE)
            hid = _glu_sparse(h2, wts["w_moe_gate"][i], wts["w_moe_up"][i], plan, EXPERT_TILE)
            ys = _down_sparse(hid, wts["w_moe_down"][i], plan, EXPERT_TILE)
            out = _post_moe(x, ys, route, plan, wts["g_post_ffn"][l], gate2, nxt)
        x, h = out if nxt is not None else (out, None)
    return x


def kernel(x_prompt, x_sample, c_prompt, c_sample, w_ada, b_ada, g_pre_mix, g_post_mix, w_in, w_fourier_out,
           w_attn_out, attn_sink, w_out, g_pre_ffn, g_post_ffn, w_dense_gate, w_dense_up, w_dense_down,
           w_router, b_router, w_moe_gate, w_moe_up, w_moe_down):
    bp, bs = c_prompt.shape[0], c_sample.shape[0]
    d = c_prompt.shape[1]
    rows = -(-(bp + bs) // 8) * 8
    c_all = jnp.zeros((rows, d), F32).at[:bp].set(c_prompt).at[bp:bp + bs].set(c_sample)
    mod = _adaln(c_all, w_ada, b_ada)

    wts = dict(
        w_in=w_in.astype(BF16), w_fourier_out=w_fourier_out.astype(BF16), w_attn_out=w_attn_out.astype(BF16),
        w_out=w_out.astype(BF16), w_dense_gate=w_dense_gate.astype(BF16), w_dense_up=w_dense_up.astype(BF16),
        w_dense_down=w_dense_down.astype(BF16), w_moe_gate=w_moe_gate.astype(BF16),
        w_moe_up=w_moe_up.astype(BF16), w_moe_down=w_moe_down.astype(BF16),
        w_router=w_router, b_router=b_router, attn_sink=attn_sink,
        g_pre_mix=g_pre_mix, g_post_mix=g_post_mix, g_pre_ffn=g_pre_ffn, g_post_ffn=g_post_ffn,
    )
    y_prompt = _trunk(x_prompt, mod[:, :bp], wts)
    y_sample = _trunk(x_sample, mod[:, bp:bp + bs], wts)
    return (y_prompt, y_sample)
```

```python
import functools

import numpy as np
import jax
import jax.numpy as jnp
from jax import lax
from jax.experimental import pallas as pl
from jax.experimental.pallas import tpu as pltpu

N_HEADS = 32
N_KV_HEADS = 8
WINDOW = 128
F_GROUPS = 16
N_EXPERTS = 8
N_MOD = 6
EPS = 1e-6

LANES = 128
BF16_SUBLANES = 16
VMEM_LIMIT_BYTES = 56 * 1024 * 1024

F32 = jnp.float32
BF16 = jnp.bfloat16
NEG_BIG = -1e30


def _params(*semantics):
    return pltpu.CompilerParams(dimension_semantics=semantics, vmem_limit_bytes=VMEM_LIMIT_BYTES)


def _sigmoid(x):
    return 0.5 * jnp.tanh(0.5 * x) + 0.5


def _tile(n, pref):
    t = min(n, pref)
    while n % t:
        t -= 1
    return t


def _adaln_kernel(c_ref, w_ref, b_ref, o_ref):
    c = c_ref[...]
    s = (c * _sigmoid(c)).astype(BF16)
    o_ref[...] = jnp.dot(s, w_ref[...].astype(BF16), preferred_element_type=F32) + b_ref[...]


def _adaln(c_all, w_ada, b_ada):
    depth, d, n = w_ada.shape
    r = c_all.shape[0]
    tn = _tile(n, 512)
    return pl.pallas_call(
        _adaln_kernel,
        grid=(depth, n // tn),
        in_specs=[
            pl.BlockSpec((r, d), lambda l, j: (0, 0)),
            pl.BlockSpec((None, d, tn), lambda l, j: (l, 0, j)),
            pl.BlockSpec((None, 1, tn), lambda l, j: (l, 0, j)),
        ],
        out_specs=pl.BlockSpec((None, r, tn), lambda l, j: (l, 0, j)),
        out_shape=jax.ShapeDtypeStruct((depth, r, n), F32),
        compiler_params=_params("parallel", "parallel"),
        name="adaln",
    )(c_all, w_ada, b_ada.reshape(depth, 1, n))


def _rms(x, g):
    return x * lax.rsqrt(jnp.mean(x * x, axis=-1, keepdims=True) + EPS) * g


def _prenorm_kernel(x_ref, g_ref, sh_ref, sc_ref, h_ref):
    h = _rms(x_ref[...], g_ref[...]) * (1.0 + sc_ref[...]) + sh_ref[...]
    h_ref[...] = h.astype(h_ref.dtype)


def _row_specs(ts, d):
    tok = pl.BlockSpec((None, ts, d), lambda b, i: (b, i, 0))
    vec = pl.BlockSpec((1, d), lambda b, i: (0, 0))
    per_seq = pl.BlockSpec((None, 1, d), lambda b, i: (b, 0, 0))
    return tok, vec, per_seq


def _prenorm(x, g, shift, scale):
    b, s, d = x.shape
    ts = _tile(s, 256)
    tok, vec, per_seq = _row_specs(ts, d)
    return pl.pallas_call(
        _prenorm_kernel,
        grid=(b, s // ts),
        in_specs=[tok, vec, per_seq, per_seq],
        out_specs=tok,
        out_shape=jax.ShapeDtypeStruct((b, s, d), BF16),
        compiler_params=_params("parallel", "parallel"),
        name="prenorm",
    )(x, g.reshape(1, d), shift, scale)


def _post_kernel(x_ref, m_ref, gp_ref, ga_ref, *rest, with_next):
    xn = x_ref[...] + ga_ref[...] * _rms(m_ref[...].astype(F32), gp_ref[...])
    if with_next:
        gn_ref, sh_ref, sc_ref, xo_ref, h_ref = rest
        xo_ref[...] = xn
        h = _rms(xn, gn_ref[...]) * (1.0 + sc_ref[...]) + sh_ref[...]
        h_ref[...] = h.astype(h_ref.dtype)
    else:
        (xo_ref,) = rest
        xo_ref[...] = xn


def _post(x, m, g_post, gate, nxt=None, h_dtype=BF16):
    b, s, d = x.shape
    ts = _tile(s, 256)
    tok, vec, per_seq = _row_specs(ts, d)
    ins = [x, m.reshape(b, s, d), g_post.reshape(1, d), gate]
    in_specs = [tok, tok, vec, per_seq]
    out_shape = [jax.ShapeDtypeStruct((b, s, d), F32)]
    out_specs = [tok]
    if nxt is not None:
        g_next, shift, scale = nxt
        ins += [g_next.reshape(1, d), shift, scale]
        in_specs += [vec, per_seq, per_seq]
        out_shape.append(jax.ShapeDtypeStruct((b, s, d), h_dtype))
        out_specs.append(tok)
    out = pl.pallas_call(
        functools.partial(_post_kernel, with_next=nxt is not None),
        grid=(b, s // ts),
        in_specs=in_specs,
        out_specs=out_specs,
        out_shape=out_shape,
        compiler_params=_params("parallel", "parallel"),
        name="post",
    )(*ins)
    return out if nxt is not None else out[0]


def _mm_kernel(a_ref, w_ref, o_ref, *scratch, nk, sigmoid_from):
    part = jnp.dot(a_ref[...], w_ref[...], preferred_element_type=F32)

    def finish(acc):
        if sigmoid_from is None:
            o_ref[...] = acc.astype(o_ref.dtype)
        else:
            j = pl.program_id(1)

            @pl.when(j < sigmoid_from)
            def _():
                o_ref[...] = acc.astype(o_ref.dtype)

            @pl.when(j >= sigmoid_from)
            def _():
                o_ref[...] = _sigmoid(acc).astype(o_ref.dtype)

    if nk == 1:
        finish(part)
        return
    (acc_ref,) = scratch
    k = pl.program_id(2)

    @pl.when(k == 0)
    def _():
        acc_ref[...] = part

    @pl.when(jnp.logical_and(k > 0, k < nk - 1))
    def _():
        acc_ref[...] += part

    @pl.when(k == nk - 1)
    def _():
        finish(acc_ref[...] + part)


def _mm(a, w, *, sigmoid_from_col=None, tm=1024, tn=1024, tk=4096):
    m, kdim = a.shape
    n = w.shape[1]
    tm, tn, tk = _tile(m, tm), _tile(n, tn), _tile(kdim, tk)
    nk = kdim // tk
    assert nk >= 1 and (nk == 1 or nk >= 2)
    sig = None
    if sigmoid_from_col is not None:
        assert sigmoid_from_col % tn == 0
        sig = sigmoid_from_col // tn
    return pl.pallas_call(
        functools.partial(_mm_kernel, nk=nk, sigmoid_from=sig),
        grid=(m // tm, n // tn, nk),
        in_specs=[
            pl.BlockSpec((tm, tk), lambda i, j, k: (i, k)),
            pl.BlockSpec((tk, tn), lambda i, j, k: (k, j)),
        ],
        out_specs=pl.BlockSpec((tm, tn), lambda i, j, k: (i, j)),
        out_shape=jax.ShapeDtypeStruct((m, n), BF16),
        scratch_shapes=[pltpu.VMEM((tm, tn), F32)] if nk > 1 else [],
        compiler_params=_params("parallel", "parallel", "arbitrary"),
        name="mm",
    )(a, w)


def _merge_kernel(y_ref, a_ref, wf_ref, wa_ref, gf_ref, ga_ref, o_ref):
    f = jnp.dot(y_ref[...], wf_ref[...], preferred_element_type=F32)
    a = jnp.dot(a_ref[...], wa_ref[...], preferred_element_type=F32)
    o = gf_ref[...].astype(F32) * f + ga_ref[...].astype(F32) * a
    o_ref[...] = o.astype(o_ref.dtype)


def _merge(y, a, p, w_f, w_a, gate_col0):
    m, f = y.shape
    q = a.shape[1]
    d = w_f.shape[1]
    tm, tn = _tile(m, 1024), _tile(d, 512)
    assert gate_col0 % tn == 0
    g0 = gate_col0 // tn
    nd = d // tn
    return pl.pallas_call(
        _merge_kernel,
        grid=(m // tm, nd),
        in_specs=[
            pl.BlockSpec((tm, f), lambda i, j: (i, 0)),
            pl.BlockSpec((tm, q), lambda i, j: (i, 0)),
            pl.BlockSpec((f, tn), lambda i, j: (0, j)),
            pl.BlockSpec((q, tn), lambda i, j: (0, j)),
            pl.BlockSpec((tm, tn), lambda i, j: (i, g0 + j)),
            pl.BlockSpec((tm, tn), lambda i, j: (i, g0 + nd + j)),
        ],
        out_specs=pl.BlockSpec((tm, tn), lambda i, j: (i, j)),
        out_shape=jax.ShapeDtypeStruct((m, d), BF16),
        compiler_params=_params("parallel", "parallel"),
        name="merge",
    )(y, a, w_f, w_a, p, p)


def _glu_kernel(h_ref, wg_ref, wu_ref, o_ref):
    h = h_ref[...]
    g = jnp.dot(h, wg_ref[...], preferred_element_type=F32)
    u = jnp.dot(h, wu_ref[...], preferred_element_type=F32)
    o_ref[...] = (g * _sigmoid(g) * u).astype(o_ref.dtype)


def _glu(h, w_gate, w_up):
    m, d = h.shape
    f = w_gate.shape[1]
    tm, tn = _tile(m, 1024), _tile(f, 512)
    wspec = pl.BlockSpec((d, tn), lambda i, j: (0, j))
    return pl.pallas_call(
        _glu_kernel,
        grid=(m // tm, f // tn),
        in_specs=[pl.BlockSpec((tm, d), lambda i, j: (i, 0)), wspec, wspec],
        out_specs=pl.BlockSpec((tm, tn), lambda i, j: (i, j)),
        out_shape=jax.ShapeDtypeStruct((m, f), BF16),
        compiler_params=_params("parallel", "parallel"),
        name="glu",
    )(h, w_gate, w_up)


ROUTE_LANES = 8
EXPERT_TILE = 256


def _router_kernel(h_ref, wt_ref, b_ref, o_ref):
    logits = lax.dot_general(wt_ref[...], h_ref[...], (((1,), (1,)), ((), ())), preferred_element_type=F32,
                             precision=lax.Precision.HIGHEST) + b_ref[...]
    ne = logits.shape[0]
    row = lax.broadcasted_iota(jnp.int32, logits.shape, 0)
    v1 = jnp.max(logits, axis=0, keepdims=True)
    i1 = jnp.min(jnp.where(logits == v1, row, ne), axis=0, keepdims=True)
    rest = jnp.where(row == i1, -jnp.inf, logits)
    v2 = jnp.max(rest, axis=0, keepdims=True)
    i2 = jnp.min(jnp.where(rest == v2, row, ne), axis=0, keepdims=True)
    e2 = jnp.exp(v2 - v1)
    w1 = 1.0 / (1.0 + e2)
    w2 = e2 / (1.0 + e2)
    out_row = lax.broadcasted_iota(jnp.int32, o_ref.shape, 0)
    o_ref[...] = jnp.where(out_row == 0, w1,
                           jnp.where(out_row == 1, w2,
                                     jnp.where(out_row == 2, i1.astype(F32),
                                               jnp.where(out_row == 3, i2.astype(F32), 0.0))))


def _router(h, w_router, b_router):
    m, d = h.shape
    ne = w_router.shape[1]
    ts = _tile(m, 512)
    route_t = pl.pallas_call(
        _router_kernel,
        grid=(m // ts,),
        in_specs=[pl.BlockSpec((ts, d), lambda i: (i, 0)),
                  pl.BlockSpec((ne, d), lambda i: (0, 0)),
                  pl.BlockSpec((ne, 1), lambda i: (0, 0))],
        out_specs=pl.BlockSpec((ROUTE_LANES, ts), lambda i: (0, i)),
        out_shape=jax.ShapeDtypeStruct((ROUTE_LANES, m), F32),
        compiler_params=_params("parallel"),
        name="router",
    )(h, w_router.T, b_router.reshape(ne, 1))
    return route_t.T


def _rank_kernel(route_ref, tri_ref, rank_ref, cnt_ref, carry, *, ne):
    @pl.when(pl.program_id(0) == 0)
    def _():
        carry[...] = jnp.zeros_like(carry)

    r = route_ref[...]
    lane = lax.broadcasted_iota(jnp.int32, (r.shape[0], 2 * ne), 1)
    sel = jnp.where(lane < ne, r[:, 2:3], r[:, 3:4] + float(ne))
    onehot = lane.astype(F32) == sel
    oh = jnp.where(onehot, 1.0, 0.0)
    before = jnp.dot(tri_ref[...], oh.astype(BF16), preferred_element_type=F32) + carry[...]
    ranks = jnp.where(onehot, before, 0.0)
    r0 = jnp.sum(jnp.where(lane < ne, ranks, 0.0), axis=-1, keepdims=True)
    r1 = jnp.sum(jnp.where(lane >= ne, ranks, 0.0), axis=-1, keepdims=True)
    out_lane = lax.broadcasted_iota(jnp.int32, rank_ref.shape, 1)
    rank_ref[...] = jnp.where(out_lane == 0, r0, jnp.where(out_lane == 1, r1, 0.0))
    carry[...] += jnp.sum(oh, axis=0, keepdims=True)
    cnt_ref[...] = carry[...]


def _rank(route, ne):
    m = route.shape[0]
    tr = _tile(m, 512)
    tri = np.tril(np.ones((tr, tr), np.float32), -1)
    return pl.pallas_call(
        functools.partial(_rank_kernel, ne=ne),
        grid=(m // tr,),
        in_specs=[pl.BlockSpec((tr, ROUTE_LANES), lambda i: (i, 0)),
                  pl.BlockSpec((tr, tr), lambda i: (0, 0))],
        out_specs=[pl.BlockSpec((tr, ROUTE_LANES), lambda i: (i, 0)),
                   pl.BlockSpec((1, 2 * ne), lambda i: (0, 0))],
        out_shape=[jax.ShapeDtypeStruct((m, ROUTE_LANES), F32),
                   jax.ShapeDtypeStruct((1, 2 * ne), F32)],
        scratch_shapes=[pltpu.VMEM((1, 2 * ne), F32)],
        compiler_params=_params("arbitrary"),
        name="rank",
    )(route, jnp.asarray(tri).astype(BF16))


def _moe_plan(route, rank, counts, ne, tm):
    m = route.shape[0]
    nt = (2 * m) // tm + ne
    c0 = counts[0, :ne].astype(jnp.int32)
    c = c0 + counts[0, ne:].astype(jnp.int32)
    padded = ((c + tm - 1) // tm) * tm
    ends = jnp.cumsum(padded)
    off = ends - padded
    i1 = route[:, 2].astype(jnp.int32)
    i2 = route[:, 3].astype(jnp.int32)
    pos0 = off[i1] + rank[:, 0].astype(jnp.int32)
    pos1 = off[i2] + c0[i2] + rank[:, 1].astype(jnp.int32)
    tok = jnp.arange(m, dtype=jnp.int32)
    src = jnp.zeros((nt * tm,), jnp.int32).at[pos0].set(tok).at[pos1].set(tok)
    tile_expert = jnp.sum(jnp.arange(nt, dtype=jnp.int32)[:, None] * tm >= ends[None, :], axis=1)
    tile_expert = jnp.minimum(tile_expert, ne - 1).astype(jnp.int32)
    n_used = (ends[-1] // tm).astype(jnp.int32).reshape(1)
    return dict(src=src, pos=jnp.concatenate([pos0, pos1]), tile_expert=tile_expert, n_used=n_used, nt=nt)


def _glu_sparse_kernel(src_ref, te_ref, nu_ref, h_hbm, wg_ref, wu_ref, o_ref, lhs0, lhs1, sem, *, tm):
    t = pl.program_id(0)
    nt = pl.num_programs(0)
    n_used = nu_ref[0]
    bufs = (lhs0, lhs1)

    def row_copy(row, buf, sl, r):
        return pltpu.make_async_copy(h_hbm.at[pl.ds(row, 1)], buf.at[pl.ds(r, 1)], sem.at[sl])

    def wait_tile(sl):
        def body(r, carry):
            row_copy(0, bufs[sl], sl, 0).wait()
            return carry
        lax.fori_loop(0, tm, body, 0, unroll=8)

    @pl.when(t == 0)
    def _():
        def body(r, carry):
            row_copy(src_ref[r], lhs0, 0, r).start()
            return carry
        lax.fori_loop(0, tm, body, 0, unroll=8)

    def used_step(sl):
        wait_tile(sl)
        nxt = jnp.minimum(t + 1, nt - 1) * tm
        x = bufs[sl][...].astype(BF16)
        for r in range(tm):
            row_copy(src_ref[nxt + r], bufs[1 - sl], 1 - sl, r).start()
        g = jnp.dot(x, wg_ref[...], preferred_element_type=F32)
        u = jnp.dot(x, wu_ref[...], preferred_element_type=F32)
        o_ref[...] = (g * _sigmoid(g) * u).astype(o_ref.dtype)

    for sl in range(2):
        @pl.when(jnp.logical_and(t < n_used, t % 2 == sl))
        def _(sl=sl):
            used_step(sl)

        @pl.when(jnp.logical_and(t == n_used - 1, t % 2 == sl))
        def _(sl=sl):
            wait_tile(1 - sl)

    @pl.when(t >= n_used)
    def _():
        o_ref[...] = jnp.zeros_like(o_ref)


def _glu_sparse(h, w_gate, w_up, plan, tm):
    m, d = h.shape
    _, _, fe = w_gate.shape
    nt = plan["nt"]
    wspec = pl.BlockSpec((None, d, fe), lambda t, src, te, nu: (te[t], 0, 0))
    return pl.pallas_call(
        functools.partial(_glu_sparse_kernel, tm=tm),
        grid_spec=pltpu.PrefetchScalarGridSpec(
            num_scalar_prefetch=3,
            grid=(nt,),
            in_specs=[pl.BlockSpec(memory_space=pl.ANY), wspec, wspec],
            out_specs=pl.BlockSpec((tm, fe), lambda t, src, te, nu: (t, 0)),
            scratch_shapes=[pltpu.VMEM((tm, d), F32), pltpu.VMEM((tm, d), F32), pltpu.SemaphoreType.DMA((2,))],
        ),
        out_shape=jax.ShapeDtypeStruct((nt * tm, fe), BF16),
        compiler_params=_params("arbitrary"),
        name="glu_sparse",
    )(plan["src"], plan["tile_expert"], plan["n_used"], h, w_gate, w_up)


def _pack_halves(y):
    half = y.shape[1] // 2
    lo = lax.bitcast_convert_type(y[:, :half].astype(BF16).astype(F32), jnp.uint32)
    hi = lax.bitcast_convert_type(y[:, half:].astype(BF16).astype(F32), jnp.uint32)
    return (lo >> 16) | (hi & jnp.uint32(0xFFFF0000))


def _unpack_halves(w):
    lo = lax.bitcast_convert_type(w << 16, F32)
    hi = lax.bitcast_convert_type(w & jnp.uint32(0xFFFF0000), F32)
    return jnp.concatenate([lo, hi], axis=1)


def _down_sparse_kernel(te_ref, nu_ref, hid_ref, w_ref, o_ref):
    t = pl.program_id(0)

    @pl.when(t < nu_ref[0])
    def _():
        o_ref[...] = _pack_halves(jnp.dot(hid_ref[...], w_ref[...], preferred_element_type=F32))

    @pl.when(t >= nu_ref[0])
    def _():
        o_ref[...] = jnp.zeros_like(o_ref)


def _down_sparse(hid, w_down, plan, tm):
    _, fe, d = w_down.shape
    nt = plan["nt"]
    return pl.pallas_call(
        _down_sparse_kernel,
        grid_spec=pltpu.PrefetchScalarGridSpec(
            num_scalar_prefetch=2,
            grid=(nt,),
            in_specs=[pl.BlockSpec((tm, fe), lambda t, te, nu: (t, 0)),
                      pl.BlockSpec((None, fe, d), lambda t, te, nu: (te[t], 0, 0))],
            out_specs=pl.BlockSpec((tm, d // 2), lambda t, te, nu: (t, 0)),
        ),
        out_shape=jax.ShapeDtypeStruct((nt * tm, d // 2), jnp.uint32),
        compiler_params=_params("arbitrary"),
        name="down_sparse",
    )(plan["tile_expert"], plan["n_used"], hid, w_down)


def _post_moe_kernel(pos_ref, x_ref, route_ref, gp_ref, ga_ref, ys_hbm, *rest, ts, m_total, with_next):
    if with_next:
        gn_ref, sh_ref, sc_ref, xo_ref, h_ref, buf0, buf1, sem = rest
    else:
        xo_ref, buf0, buf1, sem = rest
    i = pl.program_id(0)
    last = pl.num_programs(0) - 1
    bufs = (buf0, buf1)

    def row_copy(row, sl, k, r):
        return pltpu.make_async_copy(ys_hbm.at[pl.ds(row, 1)], bufs[sl].at[k, pl.ds(r, 1)], sem.at[sl])

    def wait_tile(sl):
        def body(r, carry):
            row_copy(0, sl, 0, 0).wait()
            return carry
        lax.fori_loop(0, 2 * ts, body, 0, unroll=8)

    @pl.when(i == 0)
    def _():
        def body(r, carry):
            for k in range(2):
                row_copy(pos_ref[k * m_total + r], 0, k, r).start()
            return carry
        lax.fori_loop(0, ts, body, 0, unroll=4)

    def step(sl):
        wait_tile(sl)
        nxt = jnp.minimum(i + 1, last) * ts
        route = route_ref[...]
        y = (route[:, 0:1] * _unpack_halves(bufs[sl][0]) + route[:, 1:2] * _unpack_halves(bufs[sl][1]))
        for r in range(ts):
            for k in range(2):
                row_copy(pos_ref[k * m_total + nxt + r], 1 - sl, k, r).start()
        xn = x_ref[...] + ga_ref[...] * _rms(y, gp_ref[...])
        xo_ref[...] = xn
        if with_next:
            h = _rms(xn, gn_ref[...]) * (1.0 + sc_ref[...]) + sh_ref[...]
            h_ref[...] = h.astype(h_ref.dtype)

    for sl in range(2):
        @pl.when(i % 2 == sl)
        def _(sl=sl):
            step(sl)

        @pl.when(jnp.logical_and(i == last, i % 2 == sl))
        def _(sl=sl):
            wait_tile(1 - sl)


def _post_moe(x, ys, route, plan, g_post, gate, nxt=None):
    b, s, d = x.shape
    m = b * s
    ts = _tile(s, 256)
    per_s = s // ts
    tok = pl.BlockSpec((ts, d), lambda i, pos: (i, 0))
    vec = pl.BlockSpec((1, d), lambda i, pos: (0, 0))
    per_seq = pl.BlockSpec((None, 1, d), lambda i, pos: (i // per_s, 0, 0))
    ins = [x.reshape(m, d), route, g_post.reshape(1, d), gate, ys]
    in_specs = [tok, pl.BlockSpec((ts, ROUTE_LANES), lambda i, pos: (i, 0)), vec, per_seq,
                pl.BlockSpec(memory_space=pl.ANY)]
    out_shape = [jax.ShapeDtypeStruct((m, d), F32)]
    out_specs = [tok]
    if nxt is not None:
        g_next, shift, scale = nxt
        ins += [g_next.reshape(1, d), shift, scale]
        in_specs += [vec, per_seq, per_seq]
        out_shape.append(jax.ShapeDtypeStruct((m, d), BF16))
        out_specs.append(tok)
    out = pl.pallas_call(
        functools.partial(_post_moe_kernel, ts=ts, m_total=m, with_next=nxt is not None),
        grid_spec=pltpu.PrefetchScalarGridSpec(
            num_scalar_prefetch=1,
            grid=(m // ts,),
            in_specs=in_specs,
            out_specs=out_specs,
            scratch_shapes=[pltpu.VMEM((2, ts, d // 2), jnp.uint32), pltpu.VMEM((2, ts, d // 2), jnp.uint32),
                            pltpu.SemaphoreType.DMA((2,))],
        ),
        out_shape=out_shape,
        compiler_params=_params("arbitrary"),
        name="post_moe",
    )(plan["pos"], *ins)
    if nxt is not None:
        return out[0].reshape(b, s, d), out[1].reshape(b, s, d)
    return out[0].reshape(b, s, d)


LOG2E = 1.4426950408889634


def _attn_kernel(q_ref, kp_ref, km_ref, kn_ref, vp_ref, vm_ref, vn_ref, bias_ref, sink_ref, o_ref,
                 k_scr, v_scr, *, tq, seq, group, hd, scale):
    w = WINDOW
    i = pl.program_id(2)
    k_scr[0:w] = kp_ref[...]
    k_scr[w:w + tq] = km_ref[...]
    k_scr[w + tq:] = kn_ref[...]
    v_scr[0:w, 0:hd] = vp_ref[...]
    v_scr[w:w + tq, 0:hd] = vm_ref[...]
    v_scr[w + tq:, 0:hd] = vn_ref[...]
    v_scr[:, hd:] = jnp.ones((tq + 2 * w, hd), v_scr.dtype)
    c = scale * LOG2E
    bias = bias_ref[...]
    sink = sink_ref[:, 0:1]
    nsub = tq // w
    for s in range(nsub):
        kw = k_scr[s * w:(s + 3) * w]
        vw = v_scr[s * w:(s + 3) * w]
        q = jnp.concatenate([q_ref[s * w:(s + 1) * w, h * hd:(h + 1) * hd] for h in range(group)], axis=0)
        sc = lax.dot_general(q, kw, (((1,), (1,)), ((), ())), preferred_element_type=F32) + bias
        if s == 0 or s == nsub - 1:
            key_pos = lax.broadcasted_iota(jnp.int32, (1, 3 * w), 1) + (i * tq + (s - 1) * w)
            valid = jnp.logical_and(key_pos >= 0, key_pos < seq)
            sc = sc + jnp.where(valid, 0.0, NEG_BIG)
        mx = jnp.maximum(jnp.max(sc, axis=-1, keepdims=True), sink)
        p = jnp.exp2((sc - mx) * c)
        pv = jnp.dot(p.astype(BF16), vw, preferred_element_type=F32)
        denom = pv[:, hd:hd + 1] + jnp.exp2((sink - mx) * c)
        o = (pv[:, 0:hd] / denom).astype(o_ref.dtype)
        for h in range(group):
            o_ref[s * w:(s + 1) * w, h * hd:(h + 1) * hd] = o[h * w:(h + 1) * w]


def _attention(p, sink, q_col0, k_col0, v_col0):
    b, s, _ = p.shape
    w = WINDOW
    group = N_HEADS // N_KV_HEADS
    hd = LANES
    gw = group * hd
    tq = _tile(s, 512)
    assert s % w == 0 and tq % w == 0 and q_col0 % gw == 0 and k_col0 % hd == 0 and v_col0 % hd == 0
    r = tq // w
    nblk = s // w
    qb, kb, vb = q_col0 // gw, k_col0 // hd, v_col0 // hd
    scale = float(hd) ** -0.5

    def halo_specs(c0):
        return [
            pl.BlockSpec((None, w, hd), lambda bi, g, i: (bi, jnp.maximum(i * r - 1, 0), c0 + g)),
            pl.BlockSpec((None, tq, hd), lambda bi, g, i: (bi, i, c0 + g)),
            pl.BlockSpec((None, w, hd), lambda bi, g, i: (bi, jnp.minimum((i + 1) * r, nblk - 1), c0 + g)),
        ]

    rel = np.abs(np.arange(w)[:, None] + w - np.arange(3 * w)[None, :]).astype(np.float64)
    slopes = np.exp2(-8.0 * np.arange(1, N_HEADS + 1, dtype=np.float64) / N_HEADS)
    bias = np.where(rel[None] <= WINDOW, -slopes[:, None, None] * rel[None] / scale, NEG_BIG)
    bias = bias.reshape(N_HEADS * w, 3 * w).astype(np.float32)
    sink_rows = jnp.broadcast_to((sink.astype(F32) / scale)[:, None, None], (N_HEADS, w, LANES))
    sink_rows = sink_rows.reshape(N_HEADS * w, LANES)
    return pl.pallas_call(
        functools.partial(_attn_kernel, tq=tq, seq=s, group=group, hd=hd, scale=scale),
        grid=(b, N_KV_HEADS, s // tq),
        in_specs=[pl.BlockSpec((None, tq, gw), lambda bi, g, i: (bi, i, qb + g))]
        + halo_specs(kb) + halo_specs(vb)
        + [pl.BlockSpec((group * w, 3 * w), lambda bi, g, i: (g, 0)),
           pl.BlockSpec((group * w, LANES), lambda bi, g, i: (g, 0))],
        out_specs=pl.BlockSpec((None, tq, gw), lambda bi, g, i: (bi, i, g)),
        out_shape=jax.ShapeDtypeStruct((b, s, N_HEADS * hd), BF16),
        scratch_shapes=[pltpu.VMEM((tq + 2 * w, hd), BF16), pltpu.VMEM((tq + 2 * w, 2 * hd), BF16)],
        compiler_params=_params("parallel", "parallel", "parallel"),
        name="attention",
    )(p, p, p, p, p, p, p, jnp.asarray(bias), sink_rows)


R_FFT = BF16_SUBLANES


def _fft_factors(s):
    n2 = min(256, s // R_FFT)
    n1 = s // n2
    assert n1 * n2 == s and n1 % R_FFT == 0 and n2 % R_FFT == 0
    return n1, n2


def _fft_tables(s, gdim):
    n1, n2 = _fft_factors(s)
    r = R_FFT
    k1 = np.arange(n1).reshape(1, n1, 1, 1, 1, 1)
    jj = np.arange(r).reshape(1, 1, 1, r, 1, 1)
    nn1 = np.arange(n1).reshape(1, 1, 1, 1, n1, 1)
    jc = np.arange(r).reshape(1, 1, 1, 1, 1, r)
    jb = np.arange(n2 // r).reshape(n2 // r, 1, 1, 1, 1, 1)
    ang = 2.0 * np.pi * ((nn1 * k1 * n2 + (jb * r + jj) * k1) % s) / s
    ang = np.broadcast_to(ang, (n2 // r, n1, 1, r, n1, 1))
    eye = (jj == jc)
    t1 = np.concatenate([np.cos(ang) * eye, np.sin(ang) * eye], axis=2)
    t1 = t1.reshape(n2 // r, n1 * 2 * r, n1 * r)
    k2 = np.arange(n2).reshape(n2, 1, 1)
    nn2 = (np.arange(n2 // r).reshape(1, n2 // r, 1) * r + np.arange(r).reshape(1, 1, r))
    beta = 2.0 * np.pi * ((k2 * nn2) % n2) / n2
    c2, s2 = np.cos(beta), np.sin(beta)
    row_a = np.stack([c2, -s2], axis=2)
    row_b = np.stack([s2, c2], axis=2)
    f2 = np.concatenate([row_a, row_b], axis=0).reshape(2 * n2, 2 * n2) / np.sqrt(s)
    cc = np.arange(gdim)
    phi = 2.0 * np.pi * ((cc[:, None] * cc[None, :]) % gdim) / gdim
    wc = np.concatenate([np.cos(phi), -np.sin(phi)], axis=0) / np.sqrt(gdim)
    return (jnp.asarray(t1.astype(np.float32)).astype(BF16),
            jnp.asarray(f2.astype(np.float32)).astype(BF16),
            jnp.asarray(wc.astype(np.float32)).astype(BF16))


def _fft1_kernel(t_ref, u_ref, z_ref):
    n1, r, c = u_ref.shape
    u = u_ref[...].reshape(n1 * r, c)
    z_ref[...] = jnp.dot(t_ref[...], u, preferred_element_type=F32).astype(z_ref.dtype)


def _fft2_kernel(f2_ref, wc_ref, z_ref, y_ref, scr, *, gdim):
    nj, rows, ct = z_ref.shape
    per_k1 = rows // R_FFT
    n2 = nj * R_FFT
    for kk in range(R_FFT):
        zin = z_ref[:, kk * per_k1:(kk + 1) * per_k1, :].reshape(2 * n2, ct)
        res = jnp.dot(f2_ref[...], zin, preferred_element_type=F32)
        a = res[:n2].astype(BF16)
        bm = res[n2:].astype(BF16)
        ys = []
        for g in range(ct // gdim):
            lhs = jnp.concatenate([a[:, g * gdim:(g + 1) * gdim], bm[:, g * gdim:(g + 1) * gdim]], axis=1)
            ys.append(jnp.dot(lhs, wc_ref[...], preferred_element_type=F32))
        scr[:, kk, :] = jnp.concatenate(ys, axis=1)
    y_ref[...] = scr[...].astype(y_ref.dtype)


def _fourier(p, fwidth):
    b, s, width = p.shape
    gdim = fwidth // F_GROUPS
    assert gdim == LANES
    n1, n2 = _fft_factors(s)
    r = R_FFT
    t1, f2, wc = _fft_tables(s, gdim)
    nj = n2 // r
    z = pl.pallas_call(
        _fft1_kernel,
        grid=(nj, b),
        in_specs=[
            pl.BlockSpec((None, 2 * n1 * r, n1 * r), lambda j, bi: (j, 0, 0)),
            pl.BlockSpec((None, n1, r, fwidth), lambda j, bi: (bi, 0, j, 0)),
        ],
        out_specs=pl.BlockSpec((None, None, 2 * n1 * r, fwidth), lambda j, bi: (bi, j, 0, 0)),
        out_shape=jax.ShapeDtypeStruct((b, nj, 2 * n1 * r, fwidth), BF16),
        compiler_params=_params("parallel", "parallel"),
        name="fft_stage1",
    )(t1, p.reshape(b, n1, n2, width))
    ct = _tile(fwidth, 256)
    nkb = n1 // r
    z = z.reshape(b, nj, nkb, 2 * r * r, fwidth)
    y = pl.pallas_call(
        functools.partial(_fft2_kernel, gdim=gdim),
        grid=(b, nkb, fwidth // ct),
        in_specs=[
            pl.BlockSpec((2 * n2, 2 * n2), lambda bi, kb, c: (0, 0)),
            pl.BlockSpec((2 * gdim, gdim), lambda bi, kb, c: (0, 0)),
            pl.BlockSpec((None, nj, None, 2 * r * r, ct), lambda bi, kb, c: (bi, 0, kb, 0, c)),
        ],
        out_specs=pl.BlockSpec((None, n2, None, r, ct), lambda bi, kb, c: (bi, 0, kb, 0, c)),
        out_shape=jax.ShapeDtypeStruct((b, n2, nkb, r, fwidth), BF16),
        scratch_shapes=[pltpu.VMEM((n2, r, ct), F32)],
        compiler_params=_params("parallel", "parallel", "parallel"),
        name="fft_stage2",
    )(f2, wc, z)
    return y.reshape(b, s, fwidth)


def _trunk(x, mod, wts):
    b, s, d = x.shape
    m = b * s
    depth = wts["w_in"].shape[0]
    fwidth = d // 2
    qwidth = d
    kvwidth = N_KV_HEADS * (d // N_HEADS)
    q0 = fwidth
    k0 = q0 + qwidth
    v0 = k0 + kvwidth
    g0 = v0 + kvwidth

    def mods(l):
        return [mod[l, :, k * d:(k + 1) * d].reshape(b, 1, d) for k in range(N_MOD)]

    sh1, sc1, ga1, sh2, sc2, ga2 = mods(0)
    h = _prenorm(x, wts["g_pre_mix"][0], sh1, sc1)
    for l in range(depth):
        p = _mm(h.reshape(m, d), wts["w_in"][l], sigmoid_from_col=g0)
        p3 = p.reshape(b, s, -1)
        y = _fourier(p3, fwidth)
        a = _attention(p3, wts["attn_sink"][l], q0, k0, v0)
        mg = _merge(y.reshape(m, fwidth), a.reshape(m, qwidth), p, wts["w_fourier_out"][l],
                    wts["w_attn_out"][l], g0)
        mo = _mm(mg, wts["w_out"][l])
        i = l // 2
        dense = l % 2 == 0
        x, h2 = _post(x, mo, wts["g_post_mix"][l], ga1, (wts["g_pre_ffn"][l], sh2, sc2),
                      h_dtype=BF16 if dense else F32)
        gate2 = ga2
        nxt = None
        if l + 1 < depth:
            sh1, sc1, ga1, sh2, sc2, ga2 = mods(l + 1)
            nxt = (wts["g_pre_mix"][l + 1], sh1, sc1)
        if dense:
            hid = _glu(h2.reshape(m, d), wts["w_dense_gate"][i], wts["w_dense_up"][i])
            out = _post(x, _mm(hid, wts["w_dense_down"][i]), wts["g_post_ffn"][l], gate2, nxt)
        else:
            ne = wts["w_router"].shape[-1]
            h2 = h2.reshape(m, d)
            route = _router(h2, wts["w_router"][i], wts["b_router"][i])
            rank, counts = _rank(route, ne)
            plan = _moe_plan(route, rank, counts, ne, EXPERT_TILE)
            hid = _glu_sparse(h2, wts["w_moe_gate"][i], wts["w_moe_up"][i], plan, EXPERT_TILE)
            ys = _down_sparse(hid, wts["w_moe_down"][i], plan, EXPERT_TILE)
            out = _post_moe(x, ys, route, plan, wts["g_post_ffn"][l], gate2, nxt)
        x, h = out if nxt is not None else (out, None)
    return x


def kernel(x_prompt, x_sample, c_prompt, c_sample, w_ada, b_ada, g_pre_mix, g_post_mix, w_in, w_fourier_out,
           w_attn_out, attn_sink, w_out, g_pre_ffn, g_post_ffn, w_dense_gate, w_dense_up, w_dense_down,
           w_router, b_router, w_moe_gate, w_moe_up, w_moe_down):
    bp, bs = c_prompt.shape[0], c_sample.shape[0]
    d = c_prompt.shape[1]
    rows = -(-(bp + bs) // 8) * 8
    c_all = jnp.zeros((rows, d), F32).at[:bp].set(c_prompt).at[bp:bp + bs].set(c_sample)
    mod = _adaln(c_all, w_ada, b_ada)

    wts = dict(
        w_in=w_in.astype(BF16), w_fourier_out=w_fourier_out.astype(BF16), w_attn_out=w_attn_out.astype(BF16),
        w_out=w_out.astype(BF16), w_dense_gate=w_dense_gate.astype(BF16), w_dense_up=w_dense_up.astype(BF16),
        w_dense_down=w_dense_down.astype(BF16), w_moe_gate=w_moe_gate.astype(BF16),
        w_moe_up=w_moe_up.astype(BF16), w_moe_down=w_moe_down.astype(BF16),
        w_router=w_router, b_router=b_router, attn_sink=attn_sink,
        g_pre_mix=g_pre_mix, g_post_mix=g_post_mix, g_pre_ffn=g_pre_ffn, g_post_ffn=g_post_ffn,
    )
    y_prompt = _trunk(x_prompt, mod[:, :bp], wts)
    y_sample = _trunk(x_sample, mod[:, bp:bp + bs], wts)
    return (y_prompt, y_sample)
```

```python
import functools

import numpy as np
import jax
import jax.numpy as jnp
from jax import lax
from jax.experimental import pallas as pl
from jax.experimental.pallas import tpu as pltpu

N_HEADS = 32
N_KV_HEADS = 8
WINDOW = 128
F_GROUPS = 16
N_EXPERTS = 8
N_MOD = 6
EPS = 1e-6

LANES = 128
BF16_SUBLANES = 16
VMEM_LIMIT_BYTES = 56 * 1024 * 1024

F32 = jnp.float32
BF16 = jnp.bfloat16
NEG_BIG = -1e30


def _params(*semantics):
    return pltpu.CompilerParams(dimension_semantics=semantics, vmem_limit_bytes=VMEM_LIMIT_BYTES)


def _sigmoid(x):
    return 0.5 * jnp.tanh(0.5 * x) + 0.5


def _tile(n, pref):
    t = min(n, pref)
    while n % t:
        t -= 1
    return t


def _adaln_kernel(c_ref, w_ref, b_ref, o_ref):
    c = c_ref[...]
    s = (c * _sigmoid(c)).astype(BF16)
    o_ref[...] = jnp.dot(s, w_ref[...].astype(BF16), preferred_element_type=F32) + b_ref[...]


def _adaln(c_all, w_ada, b_ada):
    depth, d, n = w_ada.shape
    r = c_all.shape[0]
    tn = _tile(n, 512)
    return pl.pallas_call(
        _adaln_kernel,
        grid=(depth, n // tn),
        in_specs=[
            pl.BlockSpec((r, d), lambda l, j: (0, 0)),
            pl.BlockSpec((None, d, tn), lambda l, j: (l, 0, j)),
            pl.BlockSpec((None, 1, tn), lambda l, j: (l, 0, j)),
        ],
        out_specs=pl.BlockSpec((None, r, tn), lambda l, j: (l, 0, j)),
        out_shape=jax.ShapeDtypeStruct((depth, r, n), F32),
        compiler_params=_params("parallel", "parallel"),
        name="adaln",
    )(c_all, w_ada, b_ada.reshape(depth, 1, n))


def _rms(x, g):
    return x * lax.rsqrt(jnp.mean(x * x, axis=-1, keepdims=True) + EPS) * g


def _prenorm_kernel(x_ref, g_ref, sh_ref, sc_ref, h_ref):
    h = _rms(x_ref[...], g_ref[...]) * (1.0 + sc_ref[...]) + sh_ref[...]
    h_ref[...] = h.astype(h_ref.dtype)


def _row_specs(ts, d):
    tok = pl.BlockSpec((None, ts, d), lambda b, i: (b, i, 0))
    vec = pl.BlockSpec((1, d), lambda b, i: (0, 0))
    per_seq = pl.BlockSpec((None, 1, d), lambda b, i: (b, 0, 0))
    return tok, vec, per_seq


def _prenorm(x, g, shift, scale):
    b, s, d = x.shape
    ts = _tile(s, 256)
    tok, vec, per_seq = _row_specs(ts, d)
    return pl.pallas_call(
        _prenorm_kernel,
        grid=(b, s // ts),
        in_specs=[tok, vec, per_seq, per_seq],
        out_specs=tok,
        out_shape=jax.ShapeDtypeStruct((b, s, d), BF16),
        compiler_params=_params("parallel", "parallel"),
        name="prenorm",
    )(x, g.reshape(1, d), shift, scale)


def _pack_halves(y):
    half = y.shape[1] // 2
    lo = lax.bitcast_convert_type(y[:, :half].astype(BF16).astype(F32), jnp.uint32)
    hi = lax.bitcast_convert_type(y[:, half:].astype(BF16).astype(F32), jnp.uint32)
    return (lo >> 16) | (hi & jnp.uint32(0xFFFF0000))


def _unpack_halves(w):
    lo = lax.bitcast_convert_type(w << 16, F32)
    hi = lax.bitcast_convert_type(w & jnp.uint32(0xFFFF0000), F32)
    return jnp.concatenate([lo, hi], axis=1)


ROUTE_LANES = 8


def _top2_route(logits):
    ne, t = logits.shape
    row = lax.broadcasted_iota(jnp.int32, logits.shape, 0)
    v1 = jnp.max(logits, axis=0, keepdims=True)
    i1 = jnp.min(jnp.where(logits == v1, row, ne), axis=0, keepdims=True)
    rest = jnp.where(row == i1, -jnp.inf, logits)
    v2 = jnp.max(rest, axis=0, keepdims=True)
    i2 = jnp.min(jnp.where(rest == v2, row, ne), axis=0, keepdims=True)
    e2 = jnp.exp(v2 - v1)
    w1 = 1.0 / (1.0 + e2)
    w2 = e2 / (1.0 + e2)
    out_row = lax.broadcasted_iota(jnp.int32, (ROUTE_LANES, t), 0)
    return jnp.where(out_row == 0, w1,
                     jnp.where(out_row == 1, w2,
                               jnp.where(out_row == 2, i1.astype(F32),
                                         jnp.where(out_row == 3, i2.astype(F32), 0.0))))


def _post_kernel(x_ref, m_ref, gp_ref, ga_ref, *rest, mode):
    xn = x_ref[...] + ga_ref[...] * _rms(m_ref[...].astype(F32), gp_ref[...])
    if mode is None:
        (xo_ref,) = rest
        xo_ref[...] = xn
        return
    gn_ref, sh_ref, sc_ref = rest[:3]
    h = _rms(xn, gn_ref[...]) * (1.0 + sc_ref[...]) + sh_ref[...]
    if mode == "next":
        xo_ref, h_ref = rest[3:]
        h_ref[...] = h.astype(h_ref.dtype)
    else:
        wt_ref, b_ref, xo_ref, h_ref, route_ref = rest[3:]
        h_ref[...] = _pack_halves(h)
        logits = lax.dot_general(wt_ref[...], h, (((1,), (1,)), ((), ())), preferred_element_type=F32,
                                 precision=lax.Precision.HIGHEST) + b_ref[...]
        route_ref[...] = _top2_route(logits)
    xo_ref[...] = xn


def _post(x, m, g_post, gate, nxt=None, router=None):
    b, s, d = x.shape
    ts = _tile(s, 256)
    per_s = s // ts
    tok, vec, per_seq = _row_specs(ts, d)
    ins = [x, m.reshape(b, s, d), g_post.reshape(1, d), gate]
    in_specs = [tok, tok, vec, per_seq]
    out_shape = [jax.ShapeDtypeStruct((b, s, d), F32)]
    out_specs = [tok]
    mode = None
    if nxt is not None:
        mode = "next"
        g_next, shift, scale = nxt
        ins += [g_next.reshape(1, d), shift, scale]
        in_specs += [vec, per_seq, per_seq]
        if router is None:
            out_shape.append(jax.ShapeDtypeStruct((b, s, d), BF16))
            out_specs.append(tok)
        else:
            mode = "route"
            w_router, b_router = router
            ne = w_router.shape[1]
            ins += [w_router.T, b_router.reshape(ne, 1)]
            in_specs += [pl.BlockSpec((ne, d), lambda bi, i: (0, 0)), pl.BlockSpec((ne, 1), lambda bi, i: (0, 0))]
            out_shape += [jax.ShapeDtypeStruct((b, s, d // 2), jnp.uint32),
                          jax.ShapeDtypeStruct((ROUTE_LANES, b * s), F32)]
            out_specs += [pl.BlockSpec((None, ts, d // 2), lambda bi, i: (bi, i, 0)),
                          pl.BlockSpec((ROUTE_LANES, ts), lambda bi, i: (0, bi * per_s + i))]
    out = pl.pallas_call(
        functools.partial(_post_kernel, mode=mode),
        grid=(b, s // ts),
        in_specs=in_specs,
        out_specs=out_specs,
        out_shape=out_shape,
        compiler_params=_params("parallel", "parallel"),
        name="post",
    )(*ins)
    return out if nxt is not None else out[0]


def _cast_job_specs(jobs, grid_ij):
    gi, gj = grid_ij
    steps = gi * gj
    ins, in_specs, out_shapes, out_specs = [], [], [], []
    for arr, layer in jobs:
        _, r, c = arr.shape
        split = None
        for cb in (1, 2, 4, 8, 16, 32):
            rb = steps // cb
            if steps % cb == 0 and r % rb == 0 and (r // rb) % BF16_SUBLANES == 0 and c % (cb * LANES) == 0:
                split = (rb, cb)
                break
        assert split is not None, (arr.shape, grid_ij)
        rb, cb = split

        def in_map(i, j, *_, layer=layer, cb=cb):
            step = i * gj + j
            return (layer, step // cb, step % cb)

        def out_map(i, j, *_, cb=cb):
            step = i * gj + j
            return (step // cb, step % cb)

        ins.append(arr)
        in_specs.append(pl.BlockSpec((None, r // rb, c // cb), in_map))
        out_shapes.append(jax.ShapeDtypeStruct((r, c), BF16))
        out_specs.append(pl.BlockSpec((r // rb, c // cb), out_map))
    return ins, in_specs, out_shapes, out_specs


def _run_cast_jobs(src_refs, dst_refs):
    for src, dst in zip(src_refs, dst_refs):
        dst[...] = src[...].astype(dst.dtype)


def _mm_kernel(a_ref, w_ref, *rest, nk, sigmoid_from, njobs):
    o_ref = rest[njobs]
    scratch = rest[2 * njobs + 1:]
    _run_cast_jobs(rest[:njobs], rest[njobs + 1:2 * njobs + 1])
    part = jnp.dot(a_ref[...], w_ref[...], preferred_element_type=F32)

    def finish(acc):
        if sigmoid_from is None:
            o_ref[...] = acc.astype(o_ref.dtype)
        else:
            j = pl.program_id(1)

            @pl.when(j < sigmoid_from)
            def _():
                o_ref[...] = acc.astype(o_ref.dtype)

            @pl.when(j >= sigmoid_from)
            def _():
                o_ref[...] = _sigmoid(acc).astype(o_ref.dtype)

    if nk == 1:
        finish(part)
        return
    (acc_ref,) = scratch
    k = pl.program_id(2)

    @pl.when(k == 0)
    def _():
        acc_ref[...] = part

    @pl.when(jnp.logical_and(k > 0, k < nk - 1))
    def _():
        acc_ref[...] += part

    @pl.when(k == nk - 1)
    def _():
        finish(acc_ref[...] + part)


def _mm(a, w, *, sigmoid_from_col=None, tm=1024, tn=1024, tk=4096, cast_jobs=()):
    m, kdim = a.shape
    n = w.shape[1]
    tm, tn, tk = _tile(m, tm), _tile(n, tn), _tile(kdim, tk)
    nk = kdim // tk
    sig = None
    if sigmoid_from_col is not None:
        assert sigmoid_from_col % tn == 0
        sig = sigmoid_from_col // tn
    grid = (m // tm, n // tn, nk)
    job_ins, job_in_specs, job_shapes, job_out_specs = _cast_job_specs(cast_jobs, grid[:2])
    out = pl.pallas_call(
        functools.partial(_mm_kernel, nk=nk, sigmoid_from=sig, njobs=len(cast_jobs)),
        grid=grid,
        in_specs=[
            pl.BlockSpec((tm, tk), lambda i, j, k: (i, k)),
            pl.BlockSpec((tk, tn), lambda i, j, k: (k, j)),
        ] + job_in_specs,
        out_specs=[pl.BlockSpec((tm, tn), lambda i, j, k: (i, j))] + job_out_specs,
        out_shape=[jax.ShapeDtypeStruct((m, n), BF16)] + job_shapes,
        scratch_shapes=[pltpu.VMEM((tm, tn), F32)] if nk > 1 else [],
        compiler_params=_params("parallel", "parallel", "arbitrary"),
        name="mm",
    )(a, w, *job_ins)
    return out[0], list(out[1:])


def _merge_kernel(y_ref, a_ref, wf_ref, wa_ref, gf_ref, ga_ref, o_ref):
    f = jnp.dot(y_ref[...], wf_ref[...], preferred_element_type=F32)
    a = jnp.dot(a_ref[...], wa_ref[...], preferred_element_type=F32)
    o = gf_ref[...].astype(F32) * f + ga_ref[...].astype(F32) * a
    o_ref[...] = o.astype(o_ref.dtype)


def _merge(y, a, p, w_f, w_a, gate_col0):
    m, f = y.shape
    q = a.shape[1]
    d = w_f.shape[1]
    tm, tn = _tile(m, 1024), _tile(d, 512)
    assert gate_col0 % tn == 0
    g0 = gate_col0 // tn
    nd = d // tn
    return pl.pallas_call(
        _merge_kernel,
        grid=(m // tm, nd),
        in_specs=[
            pl.BlockSpec((tm, f), lambda i, j: (i, 0)),
            pl.BlockSpec((tm, q), lambda i, j: (i, 0)),
            pl.BlockSpec((f, tn), lambda i, j: (0, j)),
            pl.BlockSpec((q, tn), lambda i, j: (0, j)),
            pl.BlockSpec((tm, tn), lambda i, j: (i, g0 + j)),
            pl.BlockSpec((tm, tn), lambda i, j: (i, g0 + nd + j)),
        ],
        out_specs=pl.BlockSpec((tm, tn), lambda i, j: (i, j)),
        out_shape=jax.ShapeDtypeStruct((m, d), BF16),
        compiler_params=_params("parallel", "parallel"),
        name="merge",
    )(y, a, w_f, w_a, p, p)


def _glu_kernel(h_ref, wg_ref, wu_ref, *rest, njobs):
    o_ref = rest[njobs]
    _run_cast_jobs(rest[:njobs], rest[njobs + 1:])
    h = h_ref[...]
    g = jnp.dot(h, wg_ref[...], preferred_element_type=F32)
    u = jnp.dot(h, wu_ref[...], preferred_element_type=F32)
    o_ref[...] = (g * _sigmoid(g) * u).astype(o_ref.dtype)


def _glu(h, w_gate, w_up, cast_jobs=()):
    m, d = h.shape
    f = w_gate.shape[1]
    tm, tn = _tile(m, 1024), _tile(f, 512)
    wspec = pl.BlockSpec((d, tn), lambda i, j: (0, j))
    grid = (m // tm, f // tn)
    job_ins, job_in_specs, job_shapes, job_out_specs = _cast_job_specs(cast_jobs, grid)
    out = pl.pallas_call(
        functools.partial(_glu_kernel, njobs=len(cast_jobs)),
        grid=grid,
        in_specs=[pl.BlockSpec((tm, d), lambda i, j: (i, 0)), wspec, wspec] + job_in_specs,
        out_specs=[pl.BlockSpec((tm, tn), lambda i, j: (i, j))] + job_out_specs,
        out_shape=[jax.ShapeDtypeStruct((m, f), BF16)] + job_shapes,
        compiler_params=_params("parallel", "parallel"),
        name="glu",
    )(h, w_gate, w_up, *job_ins)
    return out[0], list(out[1:])


EXPERT_TILE = 256


def _rank_kernel(route_ref, tri_ref, rank_ref, cnt_ref, carry, *, ne):
    @pl.when(pl.program_id(0) == 0)
    def _():
        carry[...] = jnp.zeros_like(carry)

    r = route_ref[...]
    lane = lax.broadcasted_iota(jnp.int32, (r.shape[0], 2 * ne), 1)
    sel = jnp.where(lane < ne, r[:, 2:3], r[:, 3:4] + float(ne))
    onehot = lane.astype(F32) == sel
    oh = jnp.where(onehot, 1.0, 0.0)
    before = jnp.dot(tri_ref[...], oh.astype(BF16), preferred_element_type=F32) + carry[...]
    ranks = jnp.where(onehot, before, 0.0)
    r0 = jnp.sum(jnp.where(lane < ne, ranks, 0.0), axis=-1, keepdims=True)
    r1 = jnp.sum(jnp.where(lane >= ne, ranks, 0.0), axis=-1, keepdims=True)
    out_lane = lax.broadcasted_iota(jnp.int32, rank_ref.shape, 1)
    rank_ref[...] = jnp.where(out_lane == 0, r0, jnp.where(out_lane == 1, r1, 0.0))
    carry[...] += jnp.sum(oh, axis=0, keepdims=True)
    cnt_ref[...] = carry[...]


def _rank(route, ne):
    m = route.shape[0]
    tr = _tile(m, 512)
    tri = np.tril(np.ones((tr, tr), np.float32), -1)
    return pl.pallas_call(
        functools.partial(_rank_kernel, ne=ne),
        grid=(m // tr,),
        in_specs=[pl.BlockSpec((tr, ROUTE_LANES), lambda i: (i, 0)),
                  pl.BlockSpec((tr, tr), lambda i: (0, 0))],
        out_specs=[pl.BlockSpec((tr, ROUTE_LANES), lambda i: (i, 0)),
                   pl.BlockSpec((1, 2 * ne), lambda i: (0, 0))],
        out_shape=[jax.ShapeDtypeStruct((m, ROUTE_LANES), F32),
                   jax.ShapeDtypeStruct((1, 2 * ne), F32)],
        scratch_shapes=[pltpu.VMEM((1, 2 * ne), F32)],
        compiler_params=_params("arbitrary"),
        name="rank",
    )(route, jnp.asarray(tri).astype(BF16))


def _moe_plan(route, rank, counts, ne, tm):
    m = route.shape[0]
    nt = (2 * m) // tm + ne
    c0 = counts[0, :ne].astype(jnp.int32)
    c = c0 + counts[0, ne:].astype(jnp.int32)
    padded = ((c + tm - 1) // tm) * tm
    ends = jnp.cumsum(padded)
    off = ends - padded
    i1 = route[:, 2].astype(jnp.int32)
    i2 = route[:, 3].astype(jnp.int32)
    pos0 = off[i1] + rank[:, 0].astype(jnp.int32)
    pos1 = off[i2] + c0[i2] + rank[:, 1].astype(jnp.int32)
    tok = jnp.arange(m, dtype=jnp.int32)
    src = jnp.zeros((nt * tm,), jnp.int32).at[pos0].set(tok).at[pos1].set(tok)
    tile_expert = jnp.sum(jnp.arange(nt, dtype=jnp.int32)[:, None] * tm >= ends[None, :], axis=1)
    tile_expert = jnp.minimum(tile_expert, ne - 1).astype(jnp.int32)
    n_used = (ends[-1] // tm).astype(jnp.int32).reshape(1)
    return dict(src=src, pos=jnp.concatenate([pos0, pos1]), tile_expert=tile_expert, n_used=n_used, nt=nt)


def _glu_sparse_kernel(src_ref, te_ref, nu_ref, h_hbm, wg_ref, wu_ref, o_ref, lhs0, lhs1, sem, *, tm):
    t = pl.program_id(0)
    nt = pl.num_programs(0)
    n_used = nu_ref[0]
    bufs = (lhs0, lhs1)

    def row_copy(row, buf, sl, r):
        return pltpu.make_async_copy(h_hbm.at[pl.ds(row, 1)], buf.at[pl.ds(r, 1)], sem.at[sl])

    def wait_tile(sl):
        def body(r, carry):
            row_copy(0, bufs[sl], sl, 0).wait()
            return carry
        lax.fori_loop(0, tm, body, 0, unroll=8)

    @pl.when(t == 0)
    def _():
        def body(r, carry):
            row_copy(src_ref[r], lhs0, 0, r).start()
            return carry
        lax.fori_loop(0, tm, body, 0, unroll=8)

    def used_step(sl):
        wait_tile(sl)
        nxt = jnp.minimum(t + 1, nt - 1) * tm
        x = _unpack_halves(bufs[sl][...]).astype(BF16)
        for r in range(tm):
            row_copy(src_ref[nxt + r], bufs[1 - sl], 1 - sl, r).start()
        g = jnp.dot(x, wg_ref[...], preferred_element_type=F32)
        u = jnp.dot(x, wu_ref[...], preferred_element_type=F32)
        o_ref[...] = (g * _sigmoid(g) * u).astype(o_ref.dtype)

    for sl in range(2):
        @pl.when(jnp.logical_and(t < n_used, t % 2 == sl))
        def _(sl=sl):
            used_step(sl)

        @pl.when(jnp.logical_and(t == n_used - 1, t % 2 == sl))
        def _(sl=sl):
            wait_tile(1 - sl)

    @pl.when(t >= n_used)
    def _():
        o_ref[...] = jnp.zeros_like(o_ref)


def _glu_sparse(h, w_gate, w_up, plan, tm):
    _, d, fe = w_gate.shape
    nt = plan["nt"]
    wspec = pl.BlockSpec((None, d, fe), lambda t, src, te, nu: (te[t], 0, 0))
    return pl.pallas_call(
        functools.partial(_glu_sparse_kernel, tm=tm),
        grid_spec=pltpu.PrefetchScalarGridSpec(
            num_scalar_prefetch=3,
            grid=(nt,),
            in_specs=[pl.BlockSpec(memory_space=pl.ANY), wspec, wspec],
            out_specs=pl.BlockSpec((tm, fe), lambda t, src, te, nu: (t, 0)),
            scratch_shapes=[pltpu.VMEM((tm, d // 2), jnp.uint32), pltpu.VMEM((tm, d // 2), jnp.uint32),
                            pltpu.SemaphoreType.DMA((2,))],
        ),
        out_shape=jax.ShapeDtypeStruct((nt * tm, fe), BF16),
        compiler_params=_params("arbitrary"),
        name="glu_sparse",
    )(plan["src"], plan["tile_expert"], plan["n_used"], h, w_gate, w_up)


def _down_sparse_kernel(te_ref, nu_ref, hid_ref, w_ref, o_ref):
    t = pl.program_id(0)

    @pl.when(t < nu_ref[0])
    def _():
        o_ref[...] = _pack_halves(jnp.dot(hid_ref[...], w_ref[...], preferred_element_type=F32))

    @pl.when(t >= nu_ref[0])
    def _():
        o_ref[...] = jnp.zeros_like(o_ref)


def _down_sparse(hid, w_down, plan, tm):
    _, fe, d = w_down.shape
    nt = plan["nt"]
    return pl.pallas_call(
        _down_sparse_kernel,
        grid_spec=pltpu.PrefetchScalarGridSpec(
            num_scalar_prefetch=2,
            grid=(nt,),
            in_specs=[pl.BlockSpec((tm, fe), lambda t, te, nu: (t, 0)),
                      pl.BlockSpec((None, fe, d), lambda t, te, nu: (te[t], 0, 0))],
            out_specs=pl.BlockSpec((tm, d // 2), lambda t, te, nu: (t, 0)),
        ),
        out_shape=jax.ShapeDtypeStruct((nt * tm, d // 2), jnp.uint32),
        compiler_params=_params("arbitrary"),
        name="down_sparse",
    )(plan["tile_expert"], plan["n_used"], hid, w_down)


def _post_moe_kernel(pos_ref, x_ref, route_ref, gp_ref, ga_ref, ys_hbm, *rest, ts, m_total, with_next):
    if with_next:
        gn_ref, sh_ref, sc_ref, xo_ref, h_ref, buf0, buf1, sem = rest
    else:
        xo_ref, buf0, buf1, sem = rest
    i = pl.program_id(0)
    last = pl.num_programs(0) - 1
    bufs = (buf0, buf1)

    def row_copy(row, sl, k, r):
        return pltpu.make_async_copy(ys_hbm.at[pl.ds(row, 1)], bufs[sl].at[k, pl.ds(r, 1)], sem.at[sl])

    def wait_tile(sl):
        def body(r, carry):
            row_copy(0, sl, 0, 0).wait()
            return carry
        lax.fori_loop(0, 2 * ts, body, 0, unroll=8)

    @pl.when(i == 0)
    def _():
        def body(r, carry):
            for k in range(2):
                row_copy(pos_ref[k * m_total + r], 0, k, r).start()
            return carry
        lax.fori_loop(0, ts, body, 0, unroll=4)

    def step(sl):
        wait_tile(sl)
        nxt = jnp.minimum(i + 1, last) * ts
        route = route_ref[...]
        y = (route[:, 0:1] * _unpack_halves(bufs[sl][0]) + route[:, 1:2] * _unpack_halves(bufs[sl][1]))
        for r in range(ts):
            for k in range(2):
                row_copy(pos_ref[k * m_total + nxt + r], 1 - sl, k, r).start()
        xn = x_ref[...] + ga_ref[...] * _rms(y, gp_ref[...])
        xo_ref[...] = xn
        if with_next:
            h = _rms(xn, gn_ref[...]) * (1.0 + sc_ref[...]) + sh_ref[...]
            h_ref[...] = h.astype(h_ref.dtype)

    for sl in range(2):
        @pl.when(i % 2 == sl)
        def _(sl=sl):
            step(sl)

        @pl.when(jnp.logical_and(i == last, i % 2 == sl))
        def _(sl=sl):
            wait_tile(1 - sl)


def _post_moe(x, ys, route, plan, g_post, gate, nxt=None):
    b, s, d = x.shape
    m = b * s
    ts = _tile(s, 256)
    per_s = s // ts
    tok = pl.BlockSpec((ts, d), lambda i, pos: (i, 0))
    vec = pl.BlockSpec((1, d), lambda i, pos: (0, 0))
    per_seq = pl.BlockSpec((None, 1, d), lambda i, pos: (i // per_s, 0, 0))
    ins = [x.reshape(m, d), route, g_post.reshape(1, d), gate, ys]
    in_specs = [tok, pl.BlockSpec((ts, ROUTE_LANES), lambda i, pos: (i, 0)), vec, per_seq,
                pl.BlockSpec(memory_space=pl.ANY)]
    out_shape = [jax.ShapeDtypeStruct((m, d), F32)]
    out_specs = [tok]
    if nxt is not None:
        g_next, shift, scale = nxt
        ins += [g_next.reshape(1, d), shift, scale]
        in_specs += [vec, per_seq, per_seq]
        out_shape.append(jax.ShapeDtypeStruct((m, d), BF16))
        out_specs.append(tok)
    out = pl.pallas_call(
        functools.partial(_post_moe_kernel, ts=ts, m_total=m, with_next=nxt is not None),
        grid_spec=pltpu.PrefetchScalarGridSpec(
            num_scalar_prefetch=1,
            grid=(m // ts,),
            in_specs=in_specs,
            out_specs=out_specs,
            scratch_shapes=[pltpu.VMEM((2, ts, d // 2), jnp.uint32), pltpu.VMEM((2, ts, d // 2), jnp.uint32),
                            pltpu.SemaphoreType.DMA((2,))],
        ),
        out_shape=out_shape,
        compiler_params=_params("arbitrary"),
        name="post_moe",
    )(plan["pos"], *ins)
    if nxt is not None:
        return out[0].reshape(b, s, d), out[1].reshape(b, s, d)
    return out[0].reshape(b, s, d)


LOG2E = 1.4426950408889634


def _attn_kernel(q_ref, kp_ref, km_ref, kn_ref, vp_ref, vm_ref, vn_ref, bias_ref, sink_ref, o_ref,
                 k_scr, v_scr, *, tq, seq, group, hd, scale):
    w = WINDOW
    i = pl.program_id(2)
    k_scr[0:w] = kp_ref[...]
    k_scr[w:w + tq] = km_ref[...]
    k_scr[w + tq:] = kn_ref[...]
    v_scr[0:w, 0:hd] = vp_ref[...]
    v_scr[w:w + tq, 0:hd] = vm_ref[...]
    v_scr[w + tq:, 0:hd] = vn_ref[...]
    v_scr[:, hd:] = jnp.ones((tq + 2 * w, hd), v_scr.dtype)
    c = scale * LOG2E
    bias = bias_ref[...]
    sink = sink_ref[:, 0:1]
    nsub = tq // w
    for s in range(nsub):
        kw = k_scr[s * w:(s + 3) * w]
        vw = v_scr[s * w:(s + 3) * w]
        q = jnp.concatenate([q_ref[s * w:(s + 1) * w, h * hd:(h + 1) * hd] for h in range(group)], axis=0)
        sc = lax.dot_general(q, kw, (((1,), (1,)), ((), ())), preferred_element_type=F32) + bias
        if s == 0 or s == nsub - 1:
            key_pos = lax.broadcasted_iota(jnp.int32, (1, 3 * w), 1) + (i * tq + (s - 1) * w)
            valid = jnp.logical_and(key_pos >= 0, key_pos < seq)
            sc = sc + jnp.where(valid, 0.0, NEG_BIG)
        mx = jnp.maximum(jnp.max(sc, axis=-1, keepdims=True), sink)
        p = jnp.exp2((sc - mx) * c)
        pv = jnp.dot(p.astype(BF16), vw, preferred_element_type=F32)
        denom = pv[:, hd:hd + 1] + jnp.exp2((sink - mx) * c)
        o = (pv[:, 0:hd] / denom).astype(o_ref.dtype)
        for h in range(group):
            o_ref[s * w:(s + 1) * w, h * hd:(h + 1) * hd] = o[h * w:(h + 1) * w]


def _attention(p, sink, q_col0, k_col0, v_col0):
    b, s, _ = p.shape
    w = WINDOW
    group = N_HEADS // N_KV_HEADS
    hd = LANES
    gw = group * hd
    tq = _tile(s, 512)
    assert s % w == 0 and tq % w == 0 and q_col0 % gw == 0 and k_col0 % hd == 0 and v_col0 % hd == 0
    r = tq // w
    nblk = s // w
    qb, kb, vb = q_col0 // gw, k_col0 // hd, v_col0 // hd
    scale = float(hd) ** -0.5

    def halo_specs(c0):
        return [
            pl.BlockSpec((None, w, hd), lambda bi, g, i: (bi, jnp.maximum(i * r - 1, 0), c0 + g)),
            pl.BlockSpec((None, tq, hd), lambda bi, g, i: (bi, i, c0 + g)),
            pl.BlockSpec((None, w, hd), lambda bi, g, i: (bi, jnp.minimum((i + 1) * r, nblk - 1), c0 + g)),
        ]

    rel = np.abs(np.arange(w)[:, None] + w - np.arange(3 * w)[None, :]).astype(np.float64)
    slopes = np.exp2(-8.0 * np.arange(1, N_HEADS + 1, dtype=np.float64) / N_HEADS)
    bias = np.where(rel[None] <= WINDOW, -slopes[:, None, None] * rel[None] / scale, NEG_BIG)
    bias = bias.reshape(N_HEADS * w, 3 * w).astype(np.float32)
    sink_rows = jnp.broadcast_to((sink.astype(F32) / scale)[:, None, None], (N_HEADS, w, LANES))
    sink_rows = sink_rows.reshape(N_HEADS * w, LANES)
    return pl.pallas_call(
        functools.partial(_attn_kernel, tq=tq, seq=s, group=group, hd=hd, scale=scale),
        grid=(b, N_KV_HEADS, s // tq),
        in_specs=[pl.BlockSpec((None, tq, gw), lambda bi, g, i: (bi, i, qb + g))]
        + halo_specs(kb) + halo_specs(vb)
        + [pl.BlockSpec((group * w, 3 * w), lambda bi, g, i: (g, 0)),
           pl.BlockSpec((group * w, LANES), lambda bi, g, i: (g, 0))],
        out_specs=pl.BlockSpec((None, tq, gw), lambda bi, g, i: (bi, i, g)),
        out_shape=jax.ShapeDtypeStruct((b, s, N_HEADS * hd), BF16),
        scratch_shapes=[pltpu.VMEM((tq + 2 * w, hd), BF16), pltpu.VMEM((tq + 2 * w, 2 * hd), BF16)],
        compiler_params=_params("parallel", "parallel", "parallel"),
        name="attention",
    )(p, p, p, p, p, p, p, jnp.asarray(bias), sink_rows)


R_FFT = BF16_SUBLANES


def _fft_factors(s):
    n2 = min(256, s // R_FFT)
    n1 = s // n2
    assert n1 * n2 == s and n1 % R_FFT == 0 and n2 % R_FFT == 0
    return n1, n2


def _fft_tables(s, gdim):
    n1, n2 = _fft_factors(s)
    r = R_FFT
    k1 = np.arange(n1).reshape(1, n1, 1, 1, 1, 1)
    jj = np.arange(r).reshape(1, 1, 1, r, 1, 1)
    nn1 = np.arange(n1).reshape(1, 1, 1, 1, n1, 1)
    jc = np.arange(r).reshape(1, 1, 1, 1, 1, r)
    jb = np.arange(n2 // r).reshape(n2 // r, 1, 1, 1, 1, 1)
    ang = 2.0 * np.pi * ((nn1 * k1 * n2 + (jb * r + jj) * k1) % s) / s
    ang = np.broadcast_to(ang, (n2 // r, n1, 1, r, n1, 1))
    eye = (jj == jc)
    t1 = np.concatenate([np.cos(ang) * eye, np.sin(ang) * eye], axis=2)
    t1 = t1.reshape(n2 // r, n1 * 2 * r, n1 * r)
    k2 = np.arange(n2).reshape(n2, 1, 1)
    nn2 = (np.arange(n2 // r).reshape(1, n2 // r, 1) * r + np.arange(r).reshape(1, 1, r))
    beta = 2.0 * np.pi * ((k2 * nn2) % n2) / n2
    c2, s2 = np.cos(beta), np.sin(beta)
    row_a = np.stack([c2, -s2], axis=2)
    row_b = np.stack([s2, c2], axis=2)
    f2 = np.concatenate([row_a, row_b], axis=0).reshape(2 * n2, 2 * n2) / np.sqrt(s)
    cc = np.arange(gdim)
    phi = 2.0 * np.pi * ((cc[:, None] * cc[None, :]) % gdim) / gdim
    wc = np.concatenate([np.cos(phi), -np.sin(phi)], axis=0) / np.sqrt(gdim)
    return (jnp.asarray(t1.astype(np.float32)).astype(BF16),
            jnp.asarray(f2.astype(np.float32)).astype(BF16),
            jnp.asarray(wc.astype(np.float32)).astype(BF16))


def _fft1_kernel(t_ref, u_ref, z_ref):
    n1, r, c = u_ref.shape
    u = u_ref[...].reshape(n1 * r, c)
    z_ref[...] = jnp.dot(t_ref[...], u, preferred_element_type=F32).astype(z_ref.dtype)


def _fft2_kernel(f2_ref, wc_ref, z_ref, y_ref, scr, *, gdim):
    nj, rows, ct = z_ref.shape
    per_k1 = rows // R_FFT
    n2 = nj * R_FFT
    for kk in range(R_FFT):
        zin = z_ref[:, kk * per_k1:(kk + 1) * per_k1, :].reshape(2 * n2, ct)
        res = jnp.dot(f2_ref[...], zin, preferred_element_type=F32)
        a = res[:n2].astype(BF16)
        bm = res[n2:].astype(BF16)
        ys = []
        for g in range(ct // gdim):
            lhs = jnp.concatenate([a[:, g * gdim:(g + 1) * gdim], bm[:, g * gdim:(g + 1) * gdim]], axis=1)
            ys.append(jnp.dot(lhs, wc_ref[...], preferred_element_type=F32))
        scr[:, kk, :] = jnp.concatenate(ys, axis=1)
    y_ref[...] = scr[...].astype(y_ref.dtype)


def _fourier(p, fwidth):
    b, s, width = p.shape
    gdim = fwidth // F_GROUPS
    assert gdim == LANES
    n1, n2 = _fft_factors(s)
    r = R_FFT
    t1, f2, wc = _fft_tables(s, gdim)
    nj = n2 // r
    z = pl.pallas_call(
        _fft1_kernel,
        grid=(nj, b),
        in_specs=[
            pl.BlockSpec((None, 2 * n1 * r, n1 * r), lambda j, bi: (j, 0, 0)),
            pl.BlockSpec((None, n1, r, fwidth), lambda j, bi: (bi, 0, j, 0)),
        ],
        out_specs=pl.BlockSpec((None, None, 2 * n1 * r, fwidth), lambda j, bi: (bi, j, 0, 0)),
        out_shape=jax.ShapeDtypeStruct((b, nj, 2 * n1 * r, fwidth), BF16),
        compiler_params=_params("parallel", "parallel"),
        name="fft_stage1",
    )(t1, p.reshape(b, n1, n2, width))
    ct = _tile(fwidth, 256)
    nkb = n1 // r
    z = z.reshape(b, nj, nkb, 2 * r * r, fwidth)
    y = pl.pallas_call(
        functools.partial(_fft2_kernel, gdim=gdim),
        grid=(b, nkb, fwidth // ct),
        in_specs=[
            pl.BlockSpec((2 * n2, 2 * n2), lambda bi, kb, c: (0, 0)),
            pl.BlockSpec((2 * gdim, gdim), lambda bi, kb, c: (0, 0)),
            pl.BlockSpec((None, nj, None, 2 * r * r, ct), lambda bi, kb, c: (bi, 0, kb, 0, c)),
        ],
        out_specs=pl.BlockSpec((None, n2, None, r, ct), lambda bi, kb, c: (bi, 0, kb, 0, c)),
        out_shape=jax.ShapeDtypeStruct((b, n2, nkb, r, fwidth), BF16),
        scratch_shapes=[pltpu.VMEM((n2, r, ct), F32)],
        compiler_params=_params("parallel", "parallel", "parallel"),
        name="fft_stage2",
    )(f2, wc, z)
    return y.reshape(b, s, fwidth)


class _Bf16Weights:
    def __init__(self, stacked):
        self.stacked = stacked
        self.ready = {}

    def get(self, name, layer):
        key = (name, layer)
        if key not in self.ready:
            self.ready[key] = self.stacked[name][layer].astype(BF16)
        return self.ready[key]

    def jobs(self, keys):
        todo = [k for k in keys if k not in self.ready]
        return todo, [(self.stacked[name], layer) for name, layer in todo]

    def done(self, todo, arrays):
        self.ready.update(zip(todo, arrays))


def _trunk(x, mod, wts, bfw):
    b, s, d = x.shape
    m = b * s
    depth = wts["g_pre_mix"].shape[0]
    fwidth = d // 2
    qwidth = d
    kvwidth = N_KV_HEADS * (d // N_HEADS)
    q0 = fwidth
    k0 = q0 + qwidth
    v0 = k0 + kvwidth
    g0 = v0 + kvwidth
    ne, fe = wts["n_experts"], wts["expert_width"]

    def mods(l):
        return [mod[l, :, k * d:(k + 1) * d].reshape(b, 1, d) for k in range(N_MOD)]

    def branch_keys(l):
        return [("w_fourier_out", l), ("w_attn_out", l), ("w_out", l)]

    sh1, sc1, ga1, sh2, sc2, ga2 = mods(0)
    h = _prenorm(x, wts["g_pre_mix"][0], sh1, sc1)
    for l in range(depth):
        i = l // 2
        dense = l % 2 == 0
        ffn_keys = ([("w_dense_gate", i), ("w_dense_up", i)] if dense
                    else [("w_moe_gate", i), ("w_moe_up", i), ("w_moe_down", i)])
        todo, jobs = bfw.jobs(branch_keys(l) + ffn_keys)
        p, cast = _mm(h.reshape(m, d), bfw.get("w_in", l), sigmoid_from_col=g0, cast_jobs=jobs)
        bfw.done(todo, cast)
        p3 = p.reshape(b, s, -1)
        y = _fourier(p3, fwidth)
        a = _attention(p3, wts["attn_sink"][l], q0, k0, v0)
        mg = _merge(y.reshape(m, fwidth), a.reshape(m, qwidth), p, bfw.get("w_fourier_out", l),
                    bfw.get("w_attn_out", l), g0)
        more = l + 1 < depth
        todo, jobs = bfw.jobs(branch_keys(l + 1) if more else [])
        mo, cast = _mm(mg, bfw.get("w_out", l), cast_jobs=jobs)
        bfw.done(todo, cast)
        router = None if dense else (wts["w_router"][i], wts["b_router"][i])
        x, h2, *route_t = _post(x, mo, wts["g_post_mix"][l], ga1, (wts["g_pre_ffn"][l], sh2, sc2), router=router)
        gate2 = ga2
        nxt = None
        if more:
            sh1, sc1, ga1, sh2, sc2, ga2 = mods(l + 1)
            nxt = (wts["g_pre_mix"][l + 1], sh1, sc1)
        if dense:
            todo, jobs = bfw.jobs([("w_dense_down", i)] + ([("w_in", l + 1)] if more else []))
            hid, cast = _glu(h2.reshape(m, d), bfw.get("w_dense_gate", i), bfw.get("w_dense_up", i), cast_jobs=jobs)
            bfw.done(todo, cast)
            yo, _ = _mm(hid, bfw.get("w_dense_down", i))
            out = _post(x, yo, wts["g_post_ffn"][l], gate2, nxt)
        else:
            route = route_t[0].T
            rank, counts = _rank(route, ne)
            plan = _moe_plan(route, rank, counts, ne, EXPERT_TILE)
            hid = _glu_sparse(h2.reshape(m, d // 2), bfw.get("w_moe_gate", i).reshape(ne, d, fe),
                              bfw.get("w_moe_up", i).reshape(ne, d, fe), plan, EXPERT_TILE)
            ys = _down_sparse(hid, bfw.get("w_moe_down", i).reshape(ne, fe, d), plan, EXPERT_TILE)
            out = _post_moe(x, ys, route, plan, wts["g_post_ffn"][l], gate2, nxt)
        x, h = out if nxt is not None else (out, None)
    return x


def kernel(x_prompt, x_sample, c_prompt, c_sample, w_ada, b_ada, g_pre_mix, g_post_mix, w_in, w_fourier_out,
           w_attn_out, attn_sink, w_out, g_pre_ffn, g_post_ffn, w_dense_gate, w_dense_up, w_dense_down,
           w_router, b_router, w_moe_gate, w_moe_up, w_moe_down):
    bp, bs = c_prompt.shape[0], c_sample.shape[0]
    d = c_prompt.shape[1]
    rows = -(-(bp + bs) // 8) * 8
    c_all = jnp.zeros((rows, d), F32).at[:bp].set(c_prompt).at[bp:bp + bs].set(c_sample)
    mod = _adaln(c_all, w_ada, b_ada)

    n_moe, ne, _, fe = w_moe_gate.shape
    bfw = _Bf16Weights(dict(
        w_in=w_in, w_fourier_out=w_fourier_out, w_attn_out=w_attn_out, w_out=w_out,
        w_dense_gate=w_dense_gate, w_dense_up=w_dense_up, w_dense_down=w_dense_down,
        w_moe_gate=w_moe_gate.reshape(n_moe, ne * d, fe), w_moe_up=w_moe_up.reshape(n_moe, ne * d, fe),
        w_moe_down=w_moe_down.reshape(n_moe, ne * fe, d),
    ))
    wts = dict(
        w_router=w_router, b_router=b_router, attn_sink=attn_sink, n_experts=ne, expert_width=fe,
        g_pre_mix=g_pre_mix, g_post_mix=g_post_mix, g_pre_ffn=g_pre_ffn, g_post_ffn=g_post_ffn,
    )
    y_prompt = _trunk(x_prompt, mod[:, :bp], wts, bfw)
    y_sample = _trunk(x_sample, mod[:, bp:bp + bs], wts, bfw)
    return (y_prompt, y_sample)
```

```python
import functools

import numpy as np
import jax
import jax.numpy as jnp
from jax import lax
from jax.experimental import pallas as pl
from jax.experimental.pallas import tpu as pltpu

N_HEADS = 32
N_KV_HEADS = 8
WINDOW = 128
F_GROUPS = 16
N_EXPERTS = 8
N_MOD = 6
EPS = 1e-6

LANES = 128
BF16_SUBLANES = 16
VMEM_LIMIT_BYTES = 56 * 1024 * 1024

F32 = jnp.float32
BF16 = jnp.bfloat16
NEG_BIG = -1e30


def _params(*semantics):
    return pltpu.CompilerParams(dimension_semantics=semantics, vmem_limit_bytes=VMEM_LIMIT_BYTES)


def _sigmoid(x):
    return 0.5 * jnp.tanh(0.5 * x) + 0.5


def _tile(n, pref):
    t = min(n, pref)
    while n % t:
        t -= 1
    return t


def _adaln_kernel(c_ref, w_ref, b_ref, o_ref):
    c = c_ref[...]
    s = (c * _sigmoid(c)).astype(BF16)
    o_ref[...] = jnp.dot(s, w_ref[...].astype(BF16), preferred_element_type=F32) + b_ref[...]


def _adaln(c_all, w_ada, b_ada):
    depth, d, n = w_ada.shape
    r = c_all.shape[0]
    tn = _tile(n, 512)
    return pl.pallas_call(
        _adaln_kernel,
        grid=(depth, n // tn),
        in_specs=[
            pl.BlockSpec((r, d), lambda l, j: (0, 0)),
            pl.BlockSpec((None, d, tn), lambda l, j: (l, 0, j)),
            pl.BlockSpec((None, 1, tn), lambda l, j: (l, 0, j)),
        ],
        out_specs=pl.BlockSpec((None, r, tn), lambda l, j: (l, 0, j)),
        out_shape=jax.ShapeDtypeStruct((depth, r, n), F32),
        compiler_params=_params("parallel", "parallel"),
        name="adaln",
    )(c_all, w_ada, b_ada.reshape(depth, 1, n))


def _rms(x, g):
    return x * lax.rsqrt(jnp.mean(x * x, axis=-1, keepdims=True) + EPS) * g


def _prenorm_kernel(x_ref, g_ref, sh_ref, sc_ref, h_ref):
    h = _rms(x_ref[...], g_ref[...]) * (1.0 + sc_ref[...]) + sh_ref[...]
    h_ref[...] = h.astype(h_ref.dtype)


def _row_specs(ts, d):
    tok = pl.BlockSpec((None, ts, d), lambda b, i: (b, i, 0))
    vec = pl.BlockSpec((1, d), lambda b, i: (0, 0))
    per_seq = pl.BlockSpec((None, 1, d), lambda b, i: (b, 0, 0))
    return tok, vec, per_seq


def _prenorm(x, g, shift, scale):
    b, s, d = x.shape
    ts = _tile(s, 256)
    tok, vec, per_seq = _row_specs(ts, d)
    return pl.pallas_call(
        _prenorm_kernel,
        grid=(b, s // ts),
        in_specs=[tok, vec, per_seq, per_seq],
        out_specs=tok,
        out_shape=jax.ShapeDtypeStruct((b, s, d), BF16),
        compiler_params=_params("parallel", "parallel"),
        name="prenorm",
    )(x, g.reshape(1, d), shift, scale)


def _pack_halves(y):
    half = y.shape[1] // 2
    lo = lax.bitcast_convert_type(y[:, :half].astype(BF16).astype(F32), jnp.uint32)
    hi = lax.bitcast_convert_type(y[:, half:].astype(BF16).astype(F32), jnp.uint32)
    return (lo >> 16) | (hi & jnp.uint32(0xFFFF0000))


def _unpack_halves(w):
    lo = lax.bitcast_convert_type(w << 16, F32)
    hi = lax.bitcast_convert_type(w & jnp.uint32(0xFFFF0000), F32)
    return jnp.concatenate([lo, hi], axis=1)


ROUTE_LANES = 8


def _top2_route(logits):
    ne, t = logits.shape
    row = lax.broadcasted_iota(jnp.int32, logits.shape, 0)
    v1 = jnp.max(logits, axis=0, keepdims=True)
    i1 = jnp.min(jnp.where(logits == v1, row, ne), axis=0, keepdims=True)
    rest = jnp.where(row == i1, -jnp.inf, logits)
    v2 = jnp.max(rest, axis=0, keepdims=True)
    i2 = jnp.min(jnp.where(rest == v2, row, ne), axis=0, keepdims=True)
    e2 = jnp.exp(v2 - v1)
    w1 = 1.0 / (1.0 + e2)
    w2 = e2 / (1.0 + e2)
    out_row = lax.broadcasted_iota(jnp.int32, (ROUTE_LANES, t), 0)
    return jnp.where(out_row == 0, w1,
                     jnp.where(out_row == 1, w2,
                               jnp.where(out_row == 2, i1.astype(F32),
                                         jnp.where(out_row == 3, i2.astype(F32), 0.0))))


def _post_kernel(x_ref, m_ref, gp_ref, ga_ref, *rest, mode):
    xn = x_ref[...] + ga_ref[...] * _rms(m_ref[...].astype(F32), gp_ref[...])
    if mode is None:
        (xo_ref,) = rest
        xo_ref[...] = xn
        return
    gn_ref, sh_ref, sc_ref = rest[:3]
    h = _rms(xn, gn_ref[...]) * (1.0 + sc_ref[...]) + sh_ref[...]
    if mode == "next":
        xo_ref, h_ref = rest[3:]
        h_ref[...] = h.astype(h_ref.dtype)
    else:
        wt_ref, b_ref, xo_ref, h_ref, route_ref = rest[3:]
        h_ref[...] = _pack_halves(h)
        logits = lax.dot_general(wt_ref[...], h, (((1,), (1,)), ((), ())), preferred_element_type=F32,
                                 precision=lax.Precision.HIGHEST) + b_ref[...]
        route_ref[...] = _top2_route(logits)
    xo_ref[...] = xn


def _post(x, m, g_post, gate, nxt=None, router=None):
    b, s, d = x.shape
    ts = _tile(s, 256)
    per_s = s // ts
    tok, vec, per_seq = _row_specs(ts, d)
    ins = [x, m.reshape(b, s, d), g_post.reshape(1, d), gate]
    in_specs = [tok, tok, vec, per_seq]
    out_shape = [jax.ShapeDtypeStruct((b, s, d), F32)]
    out_specs = [tok]
    mode = None
    if nxt is not None:
        mode = "next"
        g_next, shift, scale = nxt
        ins += [g_next.reshape(1, d), shift, scale]
        in_specs += [vec, per_seq, per_seq]
        if router is None:
            out_shape.append(jax.ShapeDtypeStruct((b, s, d), BF16))
            out_specs.append(tok)
        else:
            mode = "route"
            w_router, b_router = router
            ne = w_router.shape[1]
            ins += [w_router.T, b_router.reshape(ne, 1)]
            in_specs += [pl.BlockSpec((ne, d), lambda bi, i: (0, 0)), pl.BlockSpec((ne, 1), lambda bi, i: (0, 0))]
            out_shape += [jax.ShapeDtypeStruct((b, s, d // 2), jnp.uint32),
                          jax.ShapeDtypeStruct((ROUTE_LANES, b * s), F32)]
            out_specs += [pl.BlockSpec((None, ts, d // 2), lambda bi, i: (bi, i, 0)),
                          pl.BlockSpec((ROUTE_LANES, ts), lambda bi, i: (0, bi * per_s + i))]
    out = pl.pallas_call(
        functools.partial(_post_kernel, mode=mode),
        grid=(b, s // ts),
        in_specs=in_specs,
        out_specs=out_specs,
        out_shape=out_shape,
        compiler_params=_params("parallel", "parallel"),
        name="post",
    )(*ins)
    return out if nxt is not None else out[0]


def _cast_job_specs(jobs, grid_ij):
    gi, gj = grid_ij
    steps = gi * gj
    ins, in_specs, out_shapes, out_specs = [], [], [], []
    for arr, layer in jobs:
        _, r, c = arr.shape
        split = None
        for cb in (1, 2, 4, 8, 16, 32):
            rb = steps // cb
            if steps % cb == 0 and r % rb == 0 and (r // rb) % BF16_SUBLANES == 0 and c % (cb * LANES) == 0:
                split = (rb, cb)
                break
        assert split is not None, (arr.shape, grid_ij)
        rb, cb = split

        def in_map(i, j, *_, layer=layer, cb=cb):
            step = i * gj + j
            return (layer, step // cb, step % cb)

        def out_map(i, j, *_, cb=cb):
            step = i * gj + j
            return (step // cb, step % cb)

        ins.append(arr)
        in_specs.append(pl.BlockSpec((None, r // rb, c // cb), in_map))
        out_shapes.append(jax.ShapeDtypeStruct((r, c), BF16))
        out_specs.append(pl.BlockSpec((r // rb, c // cb), out_map))
    return ins, in_specs, out_shapes, out_specs


def _run_cast_jobs(src_refs, dst_refs):
    for src, dst in zip(src_refs, dst_refs):
        dst[...] = src[...].astype(dst.dtype)


def _mm_kernel(a_ref, w_ref, *rest, nk, sigmoid_from, njobs):
    o_ref = rest[njobs]
    scratch = rest[2 * njobs + 1:]
    _run_cast_jobs(rest[:njobs], rest[njobs + 1:2 * njobs + 1])
    part = jnp.dot(a_ref[...], w_ref[...], preferred_element_type=F32)

    def finish(acc):
        if sigmoid_from is None:
            o_ref[...] = acc.astype(o_ref.dtype)
        else:
            j = pl.program_id(1)

            @pl.when(j < sigmoid_from)
            def _():
                o_ref[...] = acc.astype(o_ref.dtype)

            @pl.when(j >= sigmoid_from)
            def _():
                o_ref[...] = _sigmoid(acc).astype(o_ref.dtype)

    if nk == 1:
        finish(part)
        return
    (acc_ref,) = scratch
    k = pl.program_id(2)

    @pl.when(k == 0)
    def _():
        acc_ref[...] = part

    @pl.when(jnp.logical_and(k > 0, k < nk - 1))
    def _():
        acc_ref[...] += part

    @pl.when(k == nk - 1)
    def _():
        finish(acc_ref[...] + part)


def _mm(a, w, *, sigmoid_from_col=None, tm=1024, tn=1024, tk=4096, cast_jobs=()):
    m, kdim = a.shape
    n = w.shape[1]
    tm, tn, tk = _tile(m, tm), _tile(n, tn), _tile(kdim, tk)
    nk = kdim // tk
    sig = None
    if sigmoid_from_col is not None:
        assert sigmoid_from_col % tn == 0
        sig = sigmoid_from_col // tn
    grid = (m // tm, n // tn, nk)
    job_ins, job_in_specs, job_shapes, job_out_specs = _cast_job_specs(cast_jobs, grid[:2])
    out = pl.pallas_call(
        functools.partial(_mm_kernel, nk=nk, sigmoid_from=sig, njobs=len(cast_jobs)),
        grid=grid,
        in_specs=[
            pl.BlockSpec((tm, tk), lambda i, j, k: (i, k)),
            pl.BlockSpec((tk, tn), lambda i, j, k: (k, j)),
        ] + job_in_specs,
        out_specs=[pl.BlockSpec((tm, tn), lambda i, j, k: (i, j))] + job_out_specs,
        out_shape=[jax.ShapeDtypeStruct((m, n), BF16)] + job_shapes,
        scratch_shapes=[pltpu.VMEM((tm, tn), F32)] if nk > 1 else [],
        compiler_params=_params("parallel", "parallel", "arbitrary"),
        name="mm",
    )(a, w, *job_ins)
    return out[0], list(out[1:])


def _merge_kernel(y_ref, a_ref, wf_ref, wa_ref, gf_ref, ga_ref, o_ref):
    f = jnp.dot(y_ref[...], wf_ref[...], preferred_element_type=F32)
    a = jnp.dot(a_ref[...], wa_ref[...], preferred_element_type=F32)
    o = gf_ref[...].astype(F32) * f + ga_ref[...].astype(F32) * a
    o_ref[...] = o.astype(o_ref.dtype)


def _merge(y, a, p, w_f, w_a, gate_col0):
    m, f = y.shape
    q = a.shape[1]
    d = w_f.shape[1]
    tm, tn = _tile(m, 1024), _tile(d, 512)
    assert gate_col0 % tn == 0
    g0 = gate_col0 // tn
    nd = d // tn
    return pl.pallas_call(
        _merge_kernel,
        grid=(m // tm, nd),
        in_specs=[
            pl.BlockSpec((tm, f), lambda i, j: (i, 0)),
            pl.BlockSpec((tm, q), lambda i, j: (i, 0)),
            pl.BlockSpec((f, tn), lambda i, j: (0, j)),
            pl.BlockSpec((q, tn), lambda i, j: (0, j)),
            pl.BlockSpec((tm, tn), lambda i, j: (i, g0 + j)),
            pl.BlockSpec((tm, tn), lambda i, j: (i, g0 + nd + j)),
        ],
        out_specs=pl.BlockSpec((tm, tn), lambda i, j: (i, j)),
        out_shape=jax.ShapeDtypeStruct((m, d), BF16),
        compiler_params=_params("parallel", "parallel"),
        name="merge",
    )(y, a, w_f, w_a, p, p)


def _glu_kernel(h_ref, wg_ref, wu_ref, *rest, njobs):
    o_ref = rest[njobs]
    _run_cast_jobs(rest[:njobs], rest[njobs + 1:])
    h = h_ref[...]
    g = jnp.dot(h, wg_ref[...], preferred_element_type=F32)
    u = jnp.dot(h, wu_ref[...], preferred_element_type=F32)
    o_ref[...] = (g * _sigmoid(g) * u).astype(o_ref.dtype)


def _glu(h, w_gate, w_up, cast_jobs=()):
    m, d = h.shape
    f = w_gate.shape[1]
    tm, tn = _tile(m, 1024), _tile(f, 512)
    wspec = pl.BlockSpec((d, tn), lambda i, j: (0, j))
    grid = (m // tm, f // tn)
    job_ins, job_in_specs, job_shapes, job_out_specs = _cast_job_specs(cast_jobs, grid)
    out = pl.pallas_call(
        functools.partial(_glu_kernel, njobs=len(cast_jobs)),
        grid=grid,
        in_specs=[pl.BlockSpec((tm, d), lambda i, j: (i, 0)), wspec, wspec] + job_in_specs,
        out_specs=[pl.BlockSpec((tm, tn), lambda i, j: (i, j))] + job_out_specs,
        out_shape=[jax.ShapeDtypeStruct((m, f), BF16)] + job_shapes,
        compiler_params=_params("parallel", "parallel"),
        name="glu",
    )(h, w_gate, w_up, *job_ins)
    return out[0], list(out[1:])


EXPERT_TILE = 256


def _rank_kernel(route_ref, tri_ref, rank_ref, cnt_ref, carry, *, ne):
    @pl.when(pl.program_id(0) == 0)
    def _():
        carry[...] = jnp.zeros_like(carry)

    r = route_ref[...]
    lane = lax.broadcasted_iota(jnp.int32, (r.shape[0], 2 * ne), 1)
    sel = jnp.where(lane < ne, r[:, 2:3], r[:, 3:4] + float(ne))
    onehot = lane.astype(F32) == sel
    oh = jnp.where(onehot, 1.0, 0.0)
    before = jnp.dot(tri_ref[...], oh.astype(BF16), preferred_element_type=F32) + carry[...]
    ranks = jnp.where(onehot, before, 0.0)
    r0 = jnp.sum(jnp.where(lane < ne, ranks, 0.0), axis=-1, keepdims=True)
    r1 = jnp.sum(jnp.where(lane >= ne, ranks, 0.0), axis=-1, keepdims=True)
    out_lane = lax.broadcasted_iota(jnp.int32, rank_ref.shape, 1)
    rank_ref[...] = jnp.where(out_lane == 0, r0, jnp.where(out_lane == 1, r1, 0.0))
    carry[...] += jnp.sum(oh, axis=0, keepdims=True)
    cnt_ref[...] = carry[...]


def _rank(route, ne):
    m = route.shape[0]
    tr = _tile(m, 512)
    tri = np.tril(np.ones((tr, tr), np.float32), -1)
    return pl.pallas_call(
        functools.partial(_rank_kernel, ne=ne),
        grid=(m // tr,),
        in_specs=[pl.BlockSpec((tr, ROUTE_LANES), lambda i: (i, 0)),
                  pl.BlockSpec((tr, tr), lambda i: (0, 0))],
        out_specs=[pl.BlockSpec((tr, ROUTE_LANES), lambda i: (i, 0)),
                   pl.BlockSpec((1, 2 * ne), lambda i: (0, 0))],
        out_shape=[jax.ShapeDtypeStruct((m, ROUTE_LANES), F32),
                   jax.ShapeDtypeStruct((1, 2 * ne), F32)],
        scratch_shapes=[pltpu.VMEM((1, 2 * ne), F32)],
        compiler_params=_params("arbitrary"),
        name="rank",
    )(route, jnp.asarray(tri).astype(BF16))


def _moe_plan(route, rank, counts, ne, tm):
    m = route.shape[0]
    nt = (2 * m) // tm + ne
    c0 = counts[0, :ne].astype(jnp.int32)
    c = c0 + counts[0, ne:].astype(jnp.int32)
    padded = ((c + tm - 1) // tm) * tm
    ends = jnp.cumsum(padded)
    off = ends - padded
    i1 = route[:, 2].astype(jnp.int32)
    i2 = route[:, 3].astype(jnp.int32)
    pos0 = off[i1] + rank[:, 0].astype(jnp.int32)
    pos1 = off[i2] + c0[i2] + rank[:, 1].astype(jnp.int32)
    tok = jnp.arange(m, dtype=jnp.int32)
    src = jnp.zeros((nt * tm,), jnp.int32).at[pos0].set(tok).at[pos1].set(tok)
    tile_expert = jnp.sum(jnp.arange(nt, dtype=jnp.int32)[:, None] * tm >= ends[None, :], axis=1)
    tile_expert = jnp.minimum(tile_expert, ne - 1).astype(jnp.int32)
    n_used = (ends[-1] // tm).astype(jnp.int32).reshape(1)
    return dict(src=src, pos=jnp.concatenate([pos0, pos1]), tile_expert=tile_expert, n_used=n_used, nt=nt)


def _glu_sparse_kernel(src_ref, te_ref, nu_ref, h_hbm, wg_ref, wu_ref, o_ref, lhs0, lhs1, sem, *, tm):
    t = pl.program_id(0)
    nt = pl.num_programs(0)
    n_used = nu_ref[0]
    bufs = (lhs0, lhs1)

    def row_copy(row, buf, sl, r):
        return pltpu.make_async_copy(h_hbm.at[pl.ds(row, 1)], buf.at[pl.ds(r, 1)], sem.at[sl])

    def wait_tile(sl):
        def body(r, carry):
            row_copy(0, bufs[sl], sl, 0).wait()
            return carry
        lax.fori_loop(0, tm, body, 0, unroll=8)

    @pl.when(t == 0)
    def _():
        def body(r, carry):
            row_copy(src_ref[r], lhs0, 0, r).start()
            return carry
        lax.fori_loop(0, tm, body, 0, unroll=8)

    def used_step(sl):
        wait_tile(sl)
        nxt = jnp.minimum(t + 1, nt - 1) * tm
        x = _unpack_halves(bufs[sl][...]).astype(BF16)
        for r in range(tm):
            row_copy(src_ref[nxt + r], bufs[1 - sl], 1 - sl, r).start(priority=r % 2)
        g = jnp.dot(x, wg_ref[...], preferred_element_type=F32)
        u = jnp.dot(x, wu_ref[...], preferred_element_type=F32)
        o_ref[...] = (g * _sigmoid(g) * u).astype(o_ref.dtype)

    for sl in range(2):
        @pl.when(jnp.logical_and(t < n_used, t % 2 == sl))
        def _(sl=sl):
            used_step(sl)

        @pl.when(jnp.logical_and(t == n_used - 1, t % 2 == sl))
        def _(sl=sl):
            wait_tile(1 - sl)

    @pl.when(t >= n_used)
    def _():
        o_ref[...] = jnp.zeros_like(o_ref)


def _glu_sparse(h, w_gate, w_up, plan, tm):
    _, d, fe = w_gate.shape
    nt = plan["nt"]
    wspec = pl.BlockSpec((None, d, fe), lambda t, src, te, nu: (te[t], 0, 0))
    return pl.pallas_call(
        functools.partial(_glu_sparse_kernel, tm=tm),
        grid_spec=pltpu.PrefetchScalarGridSpec(
            num_scalar_prefetch=3,
            grid=(nt,),
            in_specs=[pl.BlockSpec(memory_space=pl.ANY), wspec, wspec],
            out_specs=pl.BlockSpec((tm, fe), lambda t, src, te, nu: (t, 0)),
            scratch_shapes=[pltpu.VMEM((tm, d // 2), jnp.uint32), pltpu.VMEM((tm, d // 2), jnp.uint32),
                            pltpu.SemaphoreType.DMA((2,))],
        ),
        out_shape=jax.ShapeDtypeStruct((nt * tm, fe), BF16),
        compiler_params=_params("arbitrary"),
        name="glu_sparse",
    )(plan["src"], plan["tile_expert"], plan["n_used"], h, w_gate, w_up)


def _down_sparse_kernel(te_ref, nu_ref, hid_ref, w_ref, o_ref):
    t = pl.program_id(0)

    @pl.when(t < nu_ref[0])
    def _():
        o_ref[...] = _pack_halves(jnp.dot(hid_ref[...], w_ref[...], preferred_element_type=F32))

    @pl.when(t >= nu_ref[0])
    def _():
        o_ref[...] = jnp.zeros_like(o_ref)


def _down_sparse(hid, w_down, plan, tm):
    _, fe, d = w_down.shape
    nt = plan["nt"]
    return pl.pallas_call(
        _down_sparse_kernel,
        grid_spec=pltpu.PrefetchScalarGridSpec(
            num_scalar_prefetch=2,
            grid=(nt,),
            in_specs=[pl.BlockSpec((tm, fe), lambda t, te, nu: (t, 0)),
                      pl.BlockSpec((None, fe, d), lambda t, te, nu: (te[t], 0, 0))],
            out_specs=pl.BlockSpec((tm, d // 2), lambda t, te, nu: (t, 0)),
        ),
        out_shape=jax.ShapeDtypeStruct((nt * tm, d // 2), jnp.uint32),
        compiler_params=_params("arbitrary"),
        name="down_sparse",
    )(plan["tile_expert"], plan["n_used"], hid, w_down)


def _post_moe_kernel(pos_ref, x_ref, route_ref, gp_ref, ga_ref, ys_hbm, *rest, ts, m_total, with_next):
    if with_next:
        gn_ref, sh_ref, sc_ref, xo_ref, h_ref, buf0, buf1, sem = rest
    else:
        xo_ref, buf0, buf1, sem = rest
    i = pl.program_id(0)
    last = pl.num_programs(0) - 1
    bufs = (buf0, buf1)

    def row_copy(row, sl, k, r):
        return pltpu.make_async_copy(ys_hbm.at[pl.ds(row, 1)], bufs[sl].at[k, pl.ds(r, 1)], sem.at[sl])

    def wait_tile(sl):
        def body(r, carry):
            row_copy(0, sl, 0, 0).wait()
            return carry
        lax.fori_loop(0, 2 * ts, body, 0, unroll=8)

    @pl.when(i == 0)
    def _():
        def body(r, carry):
            for k in range(2):
                row_copy(pos_ref[k * m_total + r], 0, k, r).start()
            return carry
        lax.fori_loop(0, ts, body, 0, unroll=4)

    def step(sl):
        wait_tile(sl)
        nxt = jnp.minimum(i + 1, last) * ts
        route = route_ref[...]
        y = (route[:, 0:1] * _unpack_halves(bufs[sl][0]) + route[:, 1:2] * _unpack_halves(bufs[sl][1]))
        for r in range(ts):
            for k in range(2):
                row_copy(pos_ref[k * m_total + nxt + r], 1 - sl, k, r).start()
        xn = x_ref[...] + ga_ref[...] * _rms(y, gp_ref[...])
        xo_ref[...] = xn
        if with_next:
            h = _rms(xn, gn_ref[...]) * (1.0 + sc_ref[...]) + sh_ref[...]
            h_ref[...] = h.astype(h_ref.dtype)

    for sl in range(2):
        @pl.when(i % 2 == sl)
        def _(sl=sl):
            step(sl)

        @pl.when(jnp.logical_and(i == last, i % 2 == sl))
        def _(sl=sl):
            wait_tile(1 - sl)


def _post_moe(x, ys, route, plan, g_post, gate, nxt=None):
    b, s, d = x.shape
    m = b * s
    ts = _tile(s, 256)
    per_s = s // ts
    tok = pl.BlockSpec((ts, d), lambda i, pos: (i, 0))
    vec = pl.BlockSpec((1, d), lambda i, pos: (0, 0))
    per_seq = pl.BlockSpec((None, 1, d), lambda i, pos: (i // per_s, 0, 0))
    ins = [x.reshape(m, d), route, g_post.reshape(1, d), gate, ys]
    in_specs = [tok, pl.BlockSpec((ts, ROUTE_LANES), lambda i, pos: (i, 0)), vec, per_seq,
                pl.BlockSpec(memory_space=pl.ANY)]
    out_shape = [jax.ShapeDtypeStruct((m, d), F32)]
    out_specs = [tok]
    if nxt is not None:
        g_next, shift, scale = nxt
        ins += [g_next.reshape(1, d), shift, scale]
        in_specs += [vec, per_seq, per_seq]
        out_shape.append(jax.ShapeDtypeStruct((m, d), BF16))
        out_specs.append(tok)
    out = pl.pallas_call(
        functools.partial(_post_moe_kernel, ts=ts, m_total=m, with_next=nxt is not None),
        grid_spec=pltpu.PrefetchScalarGridSpec(
            num_scalar_prefetch=1,
            grid=(m // ts,),
            in_specs=in_specs,
            out_specs=out_specs,
            scratch_shapes=[pltpu.VMEM((2, ts, d // 2), jnp.uint32), pltpu.VMEM((2, ts, d // 2), jnp.uint32),
                            pltpu.SemaphoreType.DMA((2,))],
        ),
        out_shape=out_shape,
        compiler_params=_params("arbitrary"),
        name="post_moe",
    )(plan["pos"], *ins)
    if nxt is not None:
        return out[0].reshape(b, s, d), out[1].reshape(b, s, d)
    return out[0].reshape(b, s, d)


LOG2E = 1.4426950408889634
ATTN_HEADS_PER_PASS = 4


def _attn_kernel(q_ref, kp_ref, km_ref, kn_ref, vp_ref, vm_ref, vn_ref, bias_ref, sink_ref, o_ref,
                 k_scr, v_scr, *, tq, seq, group, hd, scale):
    w = WINDOW
    i = pl.program_id(2)
    k_scr[0:w] = kp_ref[...]
    k_scr[w:w + tq] = km_ref[...]
    k_scr[w + tq:] = kn_ref[...]
    v_scr[0:w, 0:hd] = vp_ref[...]
    v_scr[w:w + tq, 0:hd] = vm_ref[...]
    v_scr[w + tq:, 0:hd] = vn_ref[...]
    v_scr[:, hd:] = jnp.ones((tq + 2 * w, hd), v_scr.dtype)
    c = scale * LOG2E
    bias = bias_ref[...]
    sink = sink_ref[:, 0:1]
    nsub = tq // w
    hpp = ATTN_HEADS_PER_PASS
    for s in range(nsub):
        kw = k_scr[s * w:(s + 3) * w]
        vw = v_scr[s * w:(s + 3) * w]
        pen = None
        if s == 0 or s == nsub - 1:
            key_pos = lax.broadcasted_iota(jnp.int32, (1, 3 * w), 1) + (i * tq + (s - 1) * w)
            valid = jnp.logical_and(key_pos >= 0, key_pos < seq)
            pen = jnp.where(valid, 0.0, NEG_BIG)
        for h0 in range(0, group, hpp):
            rows = slice(h0 * w, (h0 + hpp) * w)
            q = jnp.concatenate([q_ref[s * w:(s + 1) * w, h * hd:(h + 1) * hd] for h in range(h0, h0 + hpp)],
                                axis=0)
            sc = lax.dot_general(q, kw, (((1,), (1,)), ((), ())), preferred_element_type=F32) + bias[rows]
            if pen is not None:
                sc = sc + pen
            mx = jnp.maximum(jnp.max(sc, axis=-1, keepdims=True), sink[rows])
            p = jnp.exp2((sc - mx) * c)
            pv = jnp.dot(p.astype(BF16), vw, preferred_element_type=F32)
            denom = pv[:, hd:hd + 1] + jnp.exp2((sink[rows] - mx) * c)
            o = (pv[:, 0:hd] / denom).astype(o_ref.dtype)
            for h in range(hpp):
                o_ref[s * w:(s + 1) * w, (h0 + h) * hd:(h0 + h + 1) * hd] = o[h * w:(h + 1) * w]


def _attention(p, sink, q_col0, k_col0, v_col0):
    b, s, _ = p.shape
    w = WINDOW
    group = N_HEADS // N_KV_HEADS
    hd = LANES
    gw = group * hd
    tq = _tile(s, 512)
    assert s % w == 0 and tq % w == 0 and q_col0 % gw == 0 and k_col0 % hd == 0 and v_col0 % hd == 0
    r = tq // w
    nblk = s // w
    qb, kb, vb = q_col0 // gw, k_col0 // hd, v_col0 // hd
    scale = float(hd) ** -0.5

    def halo_specs(c0):
        return [
            pl.BlockSpec((None, w, hd), lambda bi, g, i: (bi, jnp.maximum(i * r - 1, 0), c0 + g)),
            pl.BlockSpec((None, tq, hd), lambda bi, g, i: (bi, i, c0 + g)),
            pl.BlockSpec((None, w, hd), lambda bi, g, i: (bi, jnp.minimum((i + 1) * r, nblk - 1), c0 + g)),
        ]

    rel = np.abs(np.arange(w)[:, None] + w - np.arange(3 * w)[None, :]).astype(np.float64)
    slopes = np.exp2(-8.0 * np.arange(1, N_HEADS + 1, dtype=np.float64) / N_HEADS)
    bias = np.where(rel[None] <= WINDOW, -slopes[:, None, None] * rel[None] / scale, NEG_BIG)
    bias = bias.reshape(N_HEADS * w, 3 * w).astype(np.float32)
    sink_rows = jnp.broadcast_to((sink.astype(F32) / scale)[:, None, None], (N_HEADS, w, LANES))
    sink_rows = sink_rows.reshape(N_HEADS * w, LANES)
    return pl.pallas_call(
        functools.partial(_attn_kernel, tq=tq, seq=s, group=group, hd=hd, scale=scale),
        grid=(b, N_KV_HEADS, s // tq),
        in_specs=[pl.BlockSpec((None, tq, gw), lambda bi, g, i: (bi, i, qb + g))]
        + halo_specs(kb) + halo_specs(vb)
        + [pl.BlockSpec((group * w, 3 * w), lambda bi, g, i: (g, 0)),
           pl.BlockSpec((group * w, LANES), lambda bi, g, i: (g, 0))],
        out_specs=pl.BlockSpec((None, tq, gw), lambda bi, g, i: (bi, i, g)),
        out_shape=jax.ShapeDtypeStruct((b, s, N_HEADS * hd), BF16),
        scratch_shapes=[pltpu.VMEM((tq + 2 * w, hd), BF16), pltpu.VMEM((tq + 2 * w, 2 * hd), BF16)],
        compiler_params=_params("parallel", "parallel", "parallel"),
        name="attention",
    )(p, p, p, p, p, p, p, jnp.asarray(bias), sink_rows)


R_FFT = BF16_SUBLANES


def _fft_factors(s):
    n2 = min(256, s // R_FFT)
    n1 = s // n2
    assert n1 * n2 == s and n1 % R_FFT == 0 and n2 % R_FFT == 0
    return n1, n2


def _fft_tables(s, gdim):
    n1, n2 = _fft_factors(s)
    r = R_FFT
    k1 = np.arange(n1).reshape(1, n1, 1, 1, 1, 1)
    jj = np.arange(r).reshape(1, 1, 1, r, 1, 1)
    nn1 = np.arange(n1).reshape(1, 1, 1, 1, n1, 1)
    jc = np.arange(r).reshape(1, 1, 1, 1, 1, r)
    jb = np.arange(n2 // r).reshape(n2 // r, 1, 1, 1, 1, 1)
    ang = 2.0 * np.pi * ((nn1 * k1 * n2 + (jb * r + jj) * k1) % s) / s
    ang = np.broadcast_to(ang, (n2 // r, n1, 1, r, n1, 1))
    eye = (jj == jc)
    t1 = np.concatenate([np.cos(ang) * eye, np.sin(ang) * eye], axis=2)
    t1 = t1.reshape(n2 // r, n1 * 2 * r, n1 * r)
    cc = np.arange(gdim)
    phi = 2.0 * np.pi * ((cc[:, None] * cc[None, :]) % gdim) / gdim
    wc = np.block([[np.cos(phi), np.sin(phi)], [-np.sin(phi), np.cos(phi)]]) / np.sqrt(gdim)
    k2 = np.arange(n2).reshape(n2, 1, 1)
    nn2 = (np.arange(n2 // r).reshape(1, n2 // r, 1) * r + np.arange(r).reshape(1, 1, r))
    beta = 2.0 * np.pi * ((k2 * nn2) % n2) / n2
    f2 = np.stack([np.cos(beta), -np.sin(beta)], axis=2).reshape(n2, 2 * n2) / np.sqrt(s)
    return (jnp.asarray(t1.astype(np.float32)).astype(BF16),
            jnp.asarray(wc.astype(np.float32)).astype(BF16),
            jnp.asarray(f2.astype(np.float32)).astype(BF16))


def _fft1_kernel(t_ref, wc_ref, u_ref, z_ref, *, gdim):
    n1, r, c = u_ref.shape
    u = u_ref[...].reshape(n1 * r, c)
    g = jnp.dot(t_ref[...], u, preferred_element_type=F32).astype(BF16).reshape(n1, 2, r, c)
    gr = g[:, 0].reshape(n1 * r, c)
    gi = g[:, 1].reshape(n1 * r, c)
    ps, qs = [], []
    for grp in range(c // gdim):
        cols = slice(grp * gdim, (grp + 1) * gdim)
        pq = jnp.dot(jnp.concatenate([gr[:, cols], gi[:, cols]], axis=1), wc_ref[...], preferred_element_type=F32)
        ps.append(pq[:, :gdim])
        qs.append(pq[:, gdim:])
    z_ref[:, 0] = jnp.concatenate(ps, axis=1).reshape(n1, r, c).astype(z_ref.dtype)
    z_ref[:, 1] = jnp.concatenate(qs, axis=1).reshape(n1, r, c).astype(z_ref.dtype)


def _fft2_kernel(f2_ref, z_ref, y_ref, scr):
    nj, rows, ct = z_ref.shape
    per_k1 = rows // R_FFT
    n2 = nj * R_FFT
    for kk in range(R_FFT):
        zin = z_ref[:, kk * per_k1:(kk + 1) * per_k1, :].reshape(2 * n2, ct)
        scr[:, kk, :] = jnp.dot(f2_ref[...], zin, preferred_element_type=F32)
    y_ref[...] = scr[...].astype(y_ref.dtype)


def _fourier(p, fwidth):
    b, s, width = p.shape
    gdim = fwidth // F_GROUPS
    assert gdim == LANES
    n1, n2 = _fft_factors(s)
    r = R_FFT
    t1, wc, f2 = _fft_tables(s, gdim)
    nj = n2 // r
    z = pl.pallas_call(
        functools.partial(_fft1_kernel, gdim=gdim),
        grid=(nj, b),
        in_specs=[
            pl.BlockSpec((None, 2 * n1 * r, n1 * r), lambda j, bi: (j, 0, 0)),
            pl.BlockSpec((2 * gdim, 2 * gdim), lambda j, bi: (0, 0)),
            pl.BlockSpec((None, n1, r, fwidth), lambda j, bi: (bi, 0, j, 0)),
        ],
        out_specs=pl.BlockSpec((None, None, n1, 2, r, fwidth), lambda j, bi: (bi, j, 0, 0, 0, 0)),
        out_shape=jax.ShapeDtypeStruct((b, nj, n1, 2, r, fwidth), BF16),
        compiler_params=_params("parallel", "parallel"),
        name="fft_stage1",
    )(t1, wc, p.reshape(b, n1, n2, width))
    ct = _tile(fwidth, 512)
    nkb = n1 // r
    z = z.reshape(b, nj, nkb, 2 * r * r, fwidth)
    y = pl.pallas_call(
        _fft2_kernel,
        grid=(b, nkb, fwidth // ct),
        in_specs=[
            pl.BlockSpec((n2, 2 * n2), lambda bi, kb, c: (0, 0)),
            pl.BlockSpec((None, nj, None, 2 * r * r, ct), lambda bi, kb, c: (bi, 0, kb, 0, c)),
        ],
        out_specs=pl.BlockSpec((None, n2, None, r, ct), lambda bi, kb, c: (bi, 0, kb, 0, c)),
        out_shape=jax.ShapeDtypeStruct((b, n2, nkb, r, fwidth), BF16),
        scratch_shapes=[pltpu.VMEM((n2, r, ct), F32)],
        compiler_params=_params("parallel", "parallel", "parallel"),
        name="fft_stage2",
    )(f2, z)
    return y.reshape(b, s, fwidth)


class _Bf16Weights:
    def __init__(self, stacked):
        self.stacked = stacked
        self.ready = {}

    def get(self, name, layer):
        key = (name, layer)
        if key not in self.ready:
            self.ready[key] = self.stacked[name][layer].astype(BF16)
        return self.ready[key]

    def jobs(self, keys):
        todo = [k for k in keys if k not in self.ready]
        return todo, [(self.stacked[name], layer) for name, layer in todo]

    def done(self, todo, arrays):
        self.ready.update(zip(todo, arrays))


def _trunk(x, mod, wts, bfw):
    b, s, d = x.shape
    m = b * s
    depth = wts["g_pre_mix"].shape[0]
    fwidth = d // 2
    qwidth = d
    kvwidth = N_KV_HEADS * (d // N_HEADS)
    q0 = fwidth
    k0 = q0 + qwidth
    v0 = k0 + kvwidth
    g0 = v0 + kvwidth
    ne, fe = wts["n_experts"], wts["expert_width"]

    def mods(l):
        return [mod[l, :, k * d:(k + 1) * d].reshape(b, 1, d) for k in range(N_MOD)]

    def branch_keys(l):
        return [("w_fourier_out", l), ("w_attn_out", l), ("w_out", l)]

    sh1, sc1, ga1, sh2, sc2, ga2 = mods(0)
    h = _prenorm(x, wts["g_pre_mix"][0], sh1, sc1)
    for l in range(depth):
        i = l // 2
        dense = l % 2 == 0
        ffn_keys = ([("w_dense_gate", i), ("w_dense_up", i)] if dense
                    else [("w_moe_gate", i), ("w_moe_up", i), ("w_moe_down", i)])
        todo, jobs = bfw.jobs(branch_keys(l) + ffn_keys)
        p, cast = _mm(h.reshape(m, d), bfw.get("w_in", l), sigmoid_from_col=g0, cast_jobs=jobs)
        bfw.done(todo, cast)
        p3 = p.reshape(b, s, -1)
        y = _fourier(p3, fwidth)
        a = _attention(p3, wts["attn_sink"][l], q0, k0, v0)
        mg = _merge(y.reshape(m, fwidth), a.reshape(m, qwidth), p, bfw.get("w_fourier_out", l),
                    bfw.get("w_attn_out", l), g0)
        more = l + 1 < depth
        todo, jobs = bfw.jobs(branch_keys(l + 1) if more else [])
        mo, cast = _mm(mg, bfw.get("w_out", l), cast_jobs=jobs)
        bfw.done(todo, cast)
        router = None if dense else (wts["w_router"][i], wts["b_router"][i])
        x, h2, *route_t = _post(x, mo, wts["g_post_mix"][l], ga1, (wts["g_pre_ffn"][l], sh2, sc2), router=router)
        gate2 = ga2
        nxt = None
        if more:
            sh1, sc1, ga1, sh2, sc2, ga2 = mods(l + 1)
            nxt = (wts["g_pre_mix"][l + 1], sh1, sc1)
        if dense:
            todo, jobs = bfw.jobs([("w_dense_down", i)] + ([("w_in", l + 1)] if more else []))
            hid, cast = _glu(h2.reshape(m, d), bfw.get("w_dense_gate", i), bfw.get("w_dense_up", i), cast_jobs=jobs)
            bfw.done(todo, cast)
            yo, _ = _mm(hid, bfw.get("w_dense_down", i))
            out = _post(x, yo, wts["g_post_ffn"][l], gate2, nxt)
        else:
            route = route_t[0].T
            rank, counts = _rank(route, ne)
            plan = _moe_plan(route, rank, counts, ne, EXPERT_TILE)
            hid = _glu_sparse(h2.reshape(m, d // 2), bfw.get("w_moe_gate", i).reshape(ne, d, fe),
                              bfw.get("w_moe_up", i).reshape(ne, d, fe), plan, EXPERT_TILE)
            ys = _down_sparse(hid, bfw.get("w_moe_down", i).reshape(ne, fe, d), plan, EXPERT_TILE)
            out = _post_moe(x, ys, route, plan, wts["g_post_ffn"][l], gate2, nxt)
        x, h = out if nxt is not None else (out, None)
    return x


def kernel(x_prompt, x_sample, c_prompt, c_sample, w_ada, b_ada, g_pre_mix, g_post_mix, w_in, w_fourier_out,
           w_attn_out, attn_sink, w_out, g_pre_ffn, g_post_ffn, w_dense_gate, w_dense_up, w_dense_down,
           w_router, b_router, w_moe_gate, w_moe_up, w_moe_down):
    bp, bs = c_prompt.shape[0], c_sample.shape[0]
    d = c_prompt.shape[1]
    rows = -(-(bp + bs) // 8) * 8
    c_all = jnp.zeros((rows, d), F32).at[:bp].set(c_prompt).at[bp:bp + bs].set(c_sample)
    mod = _adaln(c_all, w_ada, b_ada)

    n_moe, ne, _, fe = w_moe_gate.shape
    bfw = _Bf16Weights(dict(
        w_in=w_in, w_fourier_out=w_fourier_out, w_attn_out=w_attn_out, w_out=w_out,
        w_dense_gate=w_dense_gate, w_dense_up=w_dense_up, w_dense_down=w_dense_down,
        w_moe_gate=w_moe_gate.reshape(n_moe, ne * d, fe), w_moe_up=w_moe_up.reshape(n_moe, ne * d, fe),
        w_moe_down=w_moe_down.reshape(n_moe, ne * fe, d),
    ))
    wts = dict(
        w_router=w_router, b_router=b_router, attn_sink=attn_sink, n_experts=ne, expert_width=fe,
        g_pre_mix=g_pre_mix, g_post_mix=g_post_mix, g_pre_ffn=g_pre_ffn, g_post_ffn=g_post_ffn,
    )
    y_prompt = _trunk(x_prompt, mod[:, :bp], wts, bfw)
    y_sample = _trunk(x_sample, mod[:, bp:bp + bs], wts, bfw)
    return (y_prompt, y_sample)
```

```python
import functools

import numpy as np
import jax
import jax.numpy as jnp
from jax import lax
from jax.experimental import pallas as pl
from jax.experimental.pallas import tpu as pltpu

N_HEADS = 32
N_KV_HEADS = 8
WINDOW = 128
F_GROUPS = 16
N_EXPERTS = 8
N_MOD = 6
EPS = 1e-6

LANES = 128
BF16_SUBLANES = 16
VMEM_LIMIT_BYTES = 56 * 1024 * 1024

F32 = jnp.float32
BF16 = jnp.bfloat16
NEG_BIG = -1e30


def _params(*semantics):
    return pltpu.CompilerParams(dimension_semantics=semantics, vmem_limit_bytes=VMEM_LIMIT_BYTES)


def _sigmoid(x):
    return 0.5 * jnp.tanh(0.5 * x) + 0.5


def _tile(n, pref):
    t = min(n, pref)
    while n % t:
        t -= 1
    return t


def _adaln_kernel(c_ref, w_ref, b_ref, o_ref):
    c = c_ref[...]
    s = (c * _sigmoid(c)).astype(BF16)
    o_ref[...] = jnp.dot(s, w_ref[...].astype(BF16), preferred_element_type=F32) + b_ref[...]


def _adaln(c_all, w_ada, b_ada):
    depth, d, n = w_ada.shape
    r = c_all.shape[0]
    tn = _tile(n, 512)
    return pl.pallas_call(
        _adaln_kernel,
        grid=(depth, n // tn),
        in_specs=[
            pl.BlockSpec((r, d), lambda l, j: (0, 0)),
            pl.BlockSpec((None, d, tn), lambda l, j: (l, 0, j)),
            pl.BlockSpec((None, 1, tn), lambda l, j: (l, 0, j)),
        ],
        out_specs=pl.BlockSpec((None, r, tn), lambda l, j: (l, 0, j)),
        out_shape=jax.ShapeDtypeStruct((depth, r, n), F32),
        compiler_params=_params("parallel", "parallel"),
        name="adaln",
    )(c_all, w_ada, b_ada.reshape(depth, 1, n))


def _rms(x, g):
    return x * lax.rsqrt(jnp.mean(x * x, axis=-1, keepdims=True) + EPS) * g


def _prenorm_kernel(x_ref, g_ref, sh_ref, sc_ref, h_ref):
    h = _rms(x_ref[...], g_ref[...]) * (1.0 + sc_ref[...]) + sh_ref[...]
    h_ref[...] = h.astype(h_ref.dtype)


def _row_specs(ts, d):
    tok = pl.BlockSpec((None, ts, d), lambda b, i: (b, i, 0))
    vec = pl.BlockSpec((1, d), lambda b, i: (0, 0))
    per_seq = pl.BlockSpec((None, 1, d), lambda b, i: (b, 0, 0))
    return tok, vec, per_seq


def _prenorm(x, g, shift, scale):
    b, s, d = x.shape
    ts = _tile(s, 256)
    tok, vec, per_seq = _row_specs(ts, d)
    return pl.pallas_call(
        _prenorm_kernel,
        grid=(b, s // ts),
        in_specs=[tok, vec, per_seq, per_seq],
        out_specs=tok,
        out_shape=jax.ShapeDtypeStruct((b, s, d), BF16),
        compiler_params=_params("parallel", "parallel"),
        name="prenorm",
    )(x, g.reshape(1, d), shift, scale)


def _pack_halves(y):
    half = y.shape[1] // 2
    lo = lax.bitcast_convert_type(y[:, :half].astype(BF16).astype(F32), jnp.uint32)
    hi = lax.bitcast_convert_type(y[:, half:].astype(BF16).astype(F32), jnp.uint32)
    return (lo >> 16) | (hi & jnp.uint32(0xFFFF0000))


def _unpack_halves(w):
    lo = lax.bitcast_convert_type(w << 16, F32)
    hi = lax.bitcast_convert_type(w & jnp.uint32(0xFFFF0000), F32)
    return jnp.concatenate([lo, hi], axis=1)


ROUTE_LANES = 8


def _top2_route(logits):
    ne, t = logits.shape
    row = lax.broadcasted_iota(jnp.int32, logits.shape, 0)
    v1 = jnp.max(logits, axis=0, keepdims=True)
    i1 = jnp.min(jnp.where(logits == v1, row, ne), axis=0, keepdims=True)
    rest = jnp.where(row == i1, -jnp.inf, logits)
    v2 = jnp.max(rest, axis=0, keepdims=True)
    i2 = jnp.min(jnp.where(rest == v2, row, ne), axis=0, keepdims=True)
    e2 = jnp.exp(v2 - v1)
    w1 = 1.0 / (1.0 + e2)
    w2 = e2 / (1.0 + e2)
    out_row = lax.broadcasted_iota(jnp.int32, (ROUTE_LANES, t), 0)
    return jnp.where(out_row == 0, w1,
                     jnp.where(out_row == 1, w2,
                               jnp.where(out_row == 2, i1.astype(F32),
                                         jnp.where(out_row == 3, i2.astype(F32), 0.0))))


def _post_kernel(x_ref, m_ref, gp_ref, ga_ref, *rest, mode):
    xn = x_ref[...] + ga_ref[...] * _rms(m_ref[...].astype(F32), gp_ref[...])
    if mode is None:
        (xo_ref,) = rest
        xo_ref[...] = xn
        return
    gn_ref, sh_ref, sc_ref = rest[:3]
    h = _rms(xn, gn_ref[...]) * (1.0 + sc_ref[...]) + sh_ref[...]
    if mode == "next":
        xo_ref, h_ref = rest[3:]
        h_ref[...] = h.astype(h_ref.dtype)
    else:
        wt_ref, b_ref, xo_ref, h_ref, route_ref = rest[3:]
        h_ref[...] = _pack_halves(h)
        logits = lax.dot_general(wt_ref[...], h, (((1,), (1,)), ((), ())), preferred_element_type=F32,
                                 precision=lax.Precision.HIGHEST) + b_ref[...]
        route_ref[...] = _top2_route(logits)
    xo_ref[...] = xn


def _post(x, m, g_post, gate, nxt=None, router=None):
    b, s, d = x.shape
    ts = _tile(s, 256)
    per_s = s // ts
    tok, vec, per_seq = _row_specs(ts, d)
    ins = [x, m.reshape(b, s, d), g_post.reshape(1, d), gate]
    in_specs = [tok, tok, vec, per_seq]
    out_shape = [jax.ShapeDtypeStruct((b, s, d), F32)]
    out_specs = [tok]
    mode = None
    if nxt is not None:
        mode = "next"
        g_next, shift, scale = nxt
        ins += [g_next.reshape(1, d), shift, scale]
        in_specs += [vec, per_seq, per_seq]
        if router is None:
            out_shape.append(jax.ShapeDtypeStruct((b, s, d), BF16))
            out_specs.append(tok)
        else:
            mode = "route"
            w_router, b_router = router
            ne = w_router.shape[1]
            ins += [w_router.T, b_router.reshape(ne, 1)]
            in_specs += [pl.BlockSpec((ne, d), lambda bi, i: (0, 0)), pl.BlockSpec((ne, 1), lambda bi, i: (0, 0))]
            out_shape += [jax.ShapeDtypeStruct((b, s, d // 2), jnp.uint32),
                          jax.ShapeDtypeStruct((ROUTE_LANES, b * s), F32)]
            out_specs += [pl.BlockSpec((None, ts, d // 2), lambda bi, i: (bi, i, 0)),
                          pl.BlockSpec((ROUTE_LANES, ts), lambda bi, i: (0, bi * per_s + i))]
    out = pl.pallas_call(
        functools.partial(_post_kernel, mode=mode),
        grid=(b, s // ts),
        in_specs=in_specs,
        out_specs=out_specs,
        out_shape=out_shape,
        compiler_params=_params("parallel", "parallel"),
        name="post",
    )(*ins)
    return out if nxt is not None else out[0]


def _cast_job_specs(jobs, grid_ij):
    gi, gj = grid_ij
    steps = gi * gj
    ins, in_specs, out_shapes, out_specs = [], [], [], []
    for arr, layer in jobs:
        _, r, c = arr.shape
        split = None
        for cb in (1, 2, 4, 8, 16, 32):
            rb = steps // cb
            if steps % cb == 0 and r % rb == 0 and (r // rb) % BF16_SUBLANES == 0 and c % (cb * LANES) == 0:
                split = (rb, cb)
                break
        assert split is not None, (arr.shape, grid_ij)
        rb, cb = split

        def in_map(i, j, *_, layer=layer, cb=cb):
            step = i * gj + j
            return (layer, step // cb, step % cb)

        def out_map(i, j, *_, cb=cb):
            step = i * gj + j
            return (step // cb, step % cb)

        ins.append(arr)
        in_specs.append(pl.BlockSpec((None, r // rb, c // cb), in_map))
        out_shapes.append(jax.ShapeDtypeStruct((r, c), BF16))
        out_specs.append(pl.BlockSpec((r // rb, c // cb), out_map))
    return ins, in_specs, out_shapes, out_specs


def _run_cast_jobs(src_refs, dst_refs):
    for src, dst in zip(src_refs, dst_refs):
        dst[...] = src[...].astype(dst.dtype)


def _mm_kernel(a_ref, w_ref, *rest, nk, sigmoid_from, njobs):
    o_ref = rest[njobs]
    scratch = rest[2 * njobs + 1:]
    _run_cast_jobs(rest[:njobs], rest[njobs + 1:2 * njobs + 1])
    part = jnp.dot(a_ref[...], w_ref[...], preferred_element_type=F32)

    def finish(acc):
        if sigmoid_from is None:
            o_ref[...] = acc.astype(o_ref.dtype)
        else:
            j = pl.program_id(1)

            @pl.when(j < sigmoid_from)
            def _():
                o_ref[...] = acc.astype(o_ref.dtype)

            @pl.when(j >= sigmoid_from)
            def _():
                o_ref[...] = _sigmoid(acc).astype(o_ref.dtype)

    if nk == 1:
        finish(part)
        return
    (acc_ref,) = scratch
    k = pl.program_id(2)

    @pl.when(k == 0)
    def _():
        acc_ref[...] = part

    @pl.when(jnp.logical_and(k > 0, k < nk - 1))
    def _():
        acc_ref[...] += part

    @pl.when(k == nk - 1)
    def _():
        finish(acc_ref[...] + part)


def _mm(a, w, *, sigmoid_from_col=None, tm=1024, tn=1024, tk=4096, cast_jobs=()):
    m, kdim = a.shape
    n = w.shape[1]
    tm, tn, tk = _tile(m, tm), _tile(n, tn), _tile(kdim, tk)
    nk = kdim // tk
    sig = None
    if sigmoid_from_col is not None:
        assert sigmoid_from_col % tn == 0
        sig = sigmoid_from_col // tn
    grid = (m // tm, n // tn, nk)
    job_ins, job_in_specs, job_shapes, job_out_specs = _cast_job_specs(cast_jobs, grid[:2])
    out = pl.pallas_call(
        functools.partial(_mm_kernel, nk=nk, sigmoid_from=sig, njobs=len(cast_jobs)),
        grid=grid,
        in_specs=[
            pl.BlockSpec((tm, tk), lambda i, j, k: (i, k)),
            pl.BlockSpec((tk, tn), lambda i, j, k: (k, j)),
        ] + job_in_specs,
        out_specs=[pl.BlockSpec((tm, tn), lambda i, j, k: (i, j))] + job_out_specs,
        out_shape=[jax.ShapeDtypeStruct((m, n), BF16)] + job_shapes,
        scratch_shapes=[pltpu.VMEM((tm, tn), F32)] if nk > 1 else [],
        compiler_params=_params("parallel", "parallel", "arbitrary"),
        name="mm",
    )(a, w, *job_ins)
    return out[0], list(out[1:])


def _merge_kernel(y_ref, a_ref, wf_ref, wa_ref, gf_ref, ga_ref, o_ref):
    f = jnp.dot(y_ref[...], wf_ref[...], preferred_element_type=F32)
    a = jnp.dot(a_ref[...], wa_ref[...], preferred_element_type=F32)
    o = gf_ref[...].astype(F32) * f + ga_ref[...].astype(F32) * a
    o_ref[...] = o.astype(o_ref.dtype)


def _merge(y, a, p, w_f, w_a, gate_col0):
    m, f = y.shape
    q = a.shape[1]
    d = w_f.shape[1]
    tm, tn = _tile(m, 1024), _tile(d, 512)
    assert gate_col0 % tn == 0
    g0 = gate_col0 // tn
    nd = d // tn
    return pl.pallas_call(
        _merge_kernel,
        grid=(m // tm, nd),
        in_specs=[
            pl.BlockSpec((tm, f), lambda i, j: (i, 0)),
            pl.BlockSpec((tm, q), lambda i, j: (i, 0)),
            pl.BlockSpec((f, tn), lambda i, j: (0, j)),
            pl.BlockSpec((q, tn), lambda i, j: (0, j)),
            pl.BlockSpec((tm, tn), lambda i, j: (i, g0 + j)),
            pl.BlockSpec((tm, tn), lambda i, j: (i, g0 + nd + j)),
        ],
        out_specs=pl.BlockSpec((tm, tn), lambda i, j: (i, j)),
        out_shape=jax.ShapeDtypeStruct((m, d), BF16),
        compiler_params=_params("parallel", "parallel"),
        name="merge",
    )(y, a, w_f, w_a, p, p)


def _glu_kernel(h_ref, wg_ref, wu_ref, *rest, njobs):
    o_ref = rest[njobs]
    _run_cast_jobs(rest[:njobs], rest[njobs + 1:])
    h = h_ref[...]
    g = jnp.dot(h, wg_ref[...], preferred_element_type=F32)
    u = jnp.dot(h, wu_ref[...], preferred_element_type=F32)
    o_ref[...] = (g * _sigmoid(g) * u).astype(o_ref.dtype)


def _glu(h, w_gate, w_up, cast_jobs=()):
    m, d = h.shape
    f = w_gate.shape[1]
    tm, tn = _tile(m, 1024), _tile(f, 512)
    wspec = pl.BlockSpec((d, tn), lambda i, j: (0, j))
    grid = (m // tm, f // tn)
    job_ins, job_in_specs, job_shapes, job_out_specs = _cast_job_specs(cast_jobs, grid)
    out = pl.pallas_call(
        functools.partial(_glu_kernel, njobs=len(cast_jobs)),
        grid=grid,
        in_specs=[pl.BlockSpec((tm, d), lambda i, j: (i, 0)), wspec, wspec] + job_in_specs,
        out_specs=[pl.BlockSpec((tm, tn), lambda i, j: (i, j))] + job_out_specs,
        out_shape=[jax.ShapeDtypeStruct((m, f), BF16)] + job_shapes,
        compiler_params=_params("parallel", "parallel"),
        name="glu",
    )(h, w_gate, w_up, *job_ins)
    return out[0], list(out[1:])


EXPERT_TILE = 256


def _rank_kernel(route_ref, tri_ref, rank_ref, cnt_ref, carry, *, ne):
    @pl.when(pl.program_id(0) == 0)
    def _():
        carry[...] = jnp.zeros_like(carry)

    r = route_ref[...]
    lane = lax.broadcasted_iota(jnp.int32, (r.shape[0], 2 * ne), 1)
    sel = jnp.where(lane < ne, r[:, 2:3], r[:, 3:4] + float(ne))
    onehot = lane.astype(F32) == sel
    oh = jnp.where(onehot, 1.0, 0.0)
    before = jnp.dot(tri_ref[...], oh.astype(BF16), preferred_element_type=F32) + carry[...]
    ranks = jnp.where(onehot, before, 0.0)
    r0 = jnp.sum(jnp.where(lane < ne, ranks, 0.0), axis=-1, keepdims=True)
    r1 = jnp.sum(jnp.where(lane >= ne, ranks, 0.0), axis=-1, keepdims=True)
    out_lane = lax.broadcasted_iota(jnp.int32, rank_ref.shape, 1)
    rank_ref[...] = jnp.where(out_lane == 0, r0, jnp.where(out_lane == 1, r1, 0.0))
    carry[...] += jnp.sum(oh, axis=0, keepdims=True)
    cnt_ref[...] = carry[...]


def _rank(route, ne):
    m = route.shape[0]
    tr = _tile(m, 512)
    tri = np.tril(np.ones((tr, tr), np.float32), -1)
    return pl.pallas_call(
        functools.partial(_rank_kernel, ne=ne),
        grid=(m // tr,),
        in_specs=[pl.BlockSpec((tr, ROUTE_LANES), lambda i: (i, 0)),
                  pl.BlockSpec((tr, tr), lambda i: (0, 0))],
        out_specs=[pl.BlockSpec((tr, ROUTE_LANES), lambda i: (i, 0)),
                   pl.BlockSpec((1, 2 * ne), lambda i: (0, 0))],
        out_shape=[jax.ShapeDtypeStruct((m, ROUTE_LANES), F32),
                   jax.ShapeDtypeStruct((1, 2 * ne), F32)],
        scratch_shapes=[pltpu.VMEM((1, 2 * ne), F32)],
        compiler_params=_params("arbitrary"),
        name="rank",
    )(route, jnp.asarray(tri).astype(BF16))


def _moe_plan(route, rank, counts, ne, tm):
    m = route.shape[0]
    nt = (2 * m) // tm + ne
    c0 = counts[0, :ne].astype(jnp.int32)
    c = c0 + counts[0, ne:].astype(jnp.int32)
    padded = ((c + tm - 1) // tm) * tm
    ends = jnp.cumsum(padded)
    off = ends - padded
    i1 = route[:, 2].astype(jnp.int32)
    i2 = route[:, 3].astype(jnp.int32)
    pos0 = off[i1] + rank[:, 0].astype(jnp.int32)
    pos1 = off[i2] + c0[i2] + rank[:, 1].astype(jnp.int32)
    tok = jnp.arange(m, dtype=jnp.int32)
    src = jnp.zeros((nt * tm,), jnp.int32).at[pos0].set(tok).at[pos1].set(tok)
    tile_expert = jnp.sum(jnp.arange(nt, dtype=jnp.int32)[:, None] * tm >= ends[None, :], axis=1)
    tile_expert = jnp.minimum(tile_expert, ne - 1).astype(jnp.int32)
    n_used = (ends[-1] // tm).astype(jnp.int32).reshape(1)
    return dict(src=src, pos=jnp.concatenate([pos0, pos1]), tile_expert=tile_expert, n_used=n_used, nt=nt)


GATHER_DEPTH = 3


def _glu_sparse_kernel(src_ref, te_ref, nu_ref, h_hbm, wg_ref, wu_ref, o_ref, *scratch, tm):
    bufs, sem = scratch[:GATHER_DEPTH], scratch[GATHER_DEPTH]
    ahead = GATHER_DEPTH - 1
    t = pl.program_id(0)
    nt = pl.num_programs(0)
    n_used = nu_ref[0]

    def row_copy(row, sl, r):
        return pltpu.make_async_copy(h_hbm.at[pl.ds(row, 1)], bufs[sl].at[pl.ds(r, 1)], sem.at[sl])

    def wait_tile(sl):
        def body(r, carry):
            row_copy(0, sl, 0).wait()
            return carry
        lax.fori_loop(0, tm, body, 0, unroll=8)

    @pl.when(t == 0)
    def _():
        for sl in range(ahead):
            def body(r, carry, sl=sl):
                row_copy(src_ref[sl * tm + r], sl, r).start()
                return carry
            lax.fori_loop(0, tm, body, 0, unroll=8)

    def used_step(sl):
        wait_tile(sl)
        nxt = jnp.minimum(t + ahead, nt - 1) * tm
        dst = (sl + ahead) % GATHER_DEPTH
        x = _unpack_halves(bufs[sl][...]).astype(BF16)
        for r in range(tm):
            row_copy(src_ref[nxt + r], dst, r).start(priority=r % 2)
        g = jnp.dot(x, wg_ref[...], preferred_element_type=F32)
        u = jnp.dot(x, wu_ref[...], preferred_element_type=F32)
        o_ref[...] = (g * _sigmoid(g) * u).astype(o_ref.dtype)

    for sl in range(GATHER_DEPTH):
        @pl.when(jnp.logical_and(t < n_used, t % GATHER_DEPTH == sl))
        def _(sl=sl):
            used_step(sl)

        @pl.when(jnp.logical_and(t == n_used - 1, t % GATHER_DEPTH == sl))
        def _(sl=sl):
            for k in range(1, GATHER_DEPTH):
                wait_tile((sl + k) % GATHER_DEPTH)

    @pl.when(t >= n_used)
    def _():
        o_ref[...] = jnp.zeros_like(o_ref)


def _glu_sparse(h, w_gate, w_up, plan, tm):
    _, d, fe = w_gate.shape
    nt = plan["nt"]
    wspec = pl.BlockSpec((None, d, fe), lambda t, src, te, nu: (te[t], 0, 0))
    return pl.pallas_call(
        functools.partial(_glu_sparse_kernel, tm=tm),
        grid_spec=pltpu.PrefetchScalarGridSpec(
            num_scalar_prefetch=3,
            grid=(nt,),
            in_specs=[pl.BlockSpec(memory_space=pl.ANY), wspec, wspec],
            out_specs=pl.BlockSpec((tm, fe), lambda t, src, te, nu: (t, 0)),
            scratch_shapes=[pltpu.VMEM((tm, d // 2), jnp.uint32)] * GATHER_DEPTH
            + [pltpu.SemaphoreType.DMA((GATHER_DEPTH,))],
        ),
        out_shape=jax.ShapeDtypeStruct((nt * tm, fe), BF16),
        compiler_params=_params("arbitrary"),
        name="glu_sparse",
    )(plan["src"], plan["tile_expert"], plan["n_used"], h, w_gate, w_up)


def _down_sparse_kernel(te_ref, nu_ref, hid_ref, w_ref, o_ref):
    t = pl.program_id(0)

    @pl.when(t < nu_ref[0])
    def _():
        o_ref[...] = _pack_halves(jnp.dot(hid_ref[...], w_ref[...], preferred_element_type=F32))

    @pl.when(t >= nu_ref[0])
    def _():
        o_ref[...] = jnp.zeros_like(o_ref)


def _down_sparse(hid, w_down, plan, tm):
    _, fe, d = w_down.shape
    nt = plan["nt"]
    return pl.pallas_call(
        _down_sparse_kernel,
        grid_spec=pltpu.PrefetchScalarGridSpec(
            num_scalar_prefetch=2,
            grid=(nt,),
            in_specs=[pl.BlockSpec((tm, fe), lambda t, te, nu: (t, 0)),
                      pl.BlockSpec((None, fe, d), lambda t, te, nu: (te[t], 0, 0))],
            out_specs=pl.BlockSpec((tm, d // 2), lambda t, te, nu: (t, 0)),
        ),
        out_shape=jax.ShapeDtypeStruct((nt * tm, d // 2), jnp.uint32),
        compiler_params=_params("arbitrary"),
        name="down_sparse",
    )(plan["tile_expert"], plan["n_used"], hid, w_down)


def _post_moe_kernel(pos_ref, x_ref, route_ref, gp_ref, ga_ref, ys_hbm, *rest, ts, m_total, with_next):
    if with_next:
        gn_ref, sh_ref, sc_ref, xo_ref, h_ref, buf0, buf1, sem = rest
    else:
        xo_ref, buf0, buf1, sem = rest
    i = pl.program_id(0)
    last = pl.num_programs(0) - 1
    bufs = (buf0, buf1)

    def row_copy(row, sl, k, r):
        return pltpu.make_async_copy(ys_hbm.at[pl.ds(row, 1)], bufs[sl].at[k, pl.ds(r, 1)], sem.at[sl])

    def wait_tile(sl):
        def body(r, carry):
            row_copy(0, sl, 0, 0).wait()
            return carry
        lax.fori_loop(0, 2 * ts, body, 0, unroll=8)

    @pl.when(i == 0)
    def _():
        def body(r, carry):
            for k in range(2):
                row_copy(pos_ref[k * m_total + r], 0, k, r).start()
            return carry
        lax.fori_loop(0, ts, body, 0, unroll=4)

    def step(sl):
        wait_tile(sl)
        nxt = jnp.minimum(i + 1, last) * ts
        route = route_ref[...]
        y = (route[:, 0:1] * _unpack_halves(bufs[sl][0]) + route[:, 1:2] * _unpack_halves(bufs[sl][1]))
        for r in range(ts):
            for k in range(2):
                row_copy(pos_ref[k * m_total + nxt + r], 1 - sl, k, r).start()
        xn = x_ref[...] + ga_ref[...] * _rms(y, gp_ref[...])
        xo_ref[...] = xn
        if with_next:
            h = _rms(xn, gn_ref[...]) * (1.0 + sc_ref[...]) + sh_ref[...]
            h_ref[...] = h.astype(h_ref.dtype)

    for sl in range(2):
        @pl.when(i % 2 == sl)
        def _(sl=sl):
            step(sl)

        @pl.when(jnp.logical_and(i == last, i % 2 == sl))
        def _(sl=sl):
            wait_tile(1 - sl)


def _post_moe(x, ys, route, plan, g_post, gate, nxt=None):
    b, s, d = x.shape
    m = b * s
    ts = _tile(s, 256)
    per_s = s // ts
    tok = pl.BlockSpec((ts, d), lambda i, pos: (i, 0))
    vec = pl.BlockSpec((1, d), lambda i, pos: (0, 0))
    per_seq = pl.BlockSpec((None, 1, d), lambda i, pos: (i // per_s, 0, 0))
    ins = [x.reshape(m, d), route, g_post.reshape(1, d), gate, ys]
    in_specs = [tok, pl.BlockSpec((ts, ROUTE_LANES), lambda i, pos: (i, 0)), vec, per_seq,
                pl.BlockSpec(memory_space=pl.ANY)]
    out_shape = [jax.ShapeDtypeStruct((m, d), F32)]
    out_specs = [tok]
    if nxt is not None:
        g_next, shift, scale = nxt
        ins += [g_next.reshape(1, d), shift, scale]
        in_specs += [vec, per_seq, per_seq]
        out_shape.append(jax.ShapeDtypeStruct((m, d), BF16))
        out_specs.append(tok)
    out = pl.pallas_call(
        functools.partial(_post_moe_kernel, ts=ts, m_total=m, with_next=nxt is not None),
        grid_spec=pltpu.PrefetchScalarGridSpec(
            num_scalar_prefetch=1,
            grid=(m // ts,),
            in_specs=in_specs,
            out_specs=out_specs,
            scratch_shapes=[pltpu.VMEM((2, ts, d // 2), jnp.uint32), pltpu.VMEM((2, ts, d // 2), jnp.uint32),
                            pltpu.SemaphoreType.DMA((2,))],
        ),
        out_shape=out_shape,
        compiler_params=_params("arbitrary"),
        name="post_moe",
    )(plan["pos"], *ins)
    if nxt is not None:
        return out[0].reshape(b, s, d), out[1].reshape(b, s, d)
    return out[0].reshape(b, s, d)


LOG2E = 1.4426950408889634
ATTN_HEADS_PER_PASS = 4


def _attn_kernel(q_ref, kp_ref, km_ref, kn_ref, vp_ref, vm_ref, vn_ref, bias_ref, sink_ref, o_ref,
                 k_scr, v_scr, *, tq, seq, group, hd, scale):
    w = WINDOW
    i = pl.program_id(2)
    k_scr[0:w] = kp_ref[...]
    k_scr[w:w + tq] = km_ref[...]
    k_scr[w + tq:] = kn_ref[...]
    v_scr[0:w, 0:hd] = vp_ref[...]
    v_scr[w:w + tq, 0:hd] = vm_ref[...]
    v_scr[w + tq:, 0:hd] = vn_ref[...]
    v_scr[:, hd:] = jnp.ones((tq + 2 * w, hd), v_scr.dtype)
    c = scale * LOG2E
    bias = bias_ref[...]
    sink_rep = sink_ref[...]
    sink = sink_rep[:, 0:1]
    nsub = tq // w
    hpp = ATTN_HEADS_PER_PASS
    for s in range(nsub):
        kw = k_scr[s * w:(s + 3) * w]
        vw = v_scr[s * w:(s + 3) * w]
        pen = None
        if s == 0 or s == nsub - 1:
            key_pos = lax.broadcasted_iota(jnp.int32, (1, 3 * w), 1) + (i * tq + (s - 1) * w)
            valid = jnp.logical_and(key_pos >= 0, key_pos < seq)
            pen = jnp.where(valid, 0.0, NEG_BIG)
        for h0 in range(0, group, hpp):
            rows = slice(h0 * w, (h0 + hpp) * w)
            q = jnp.concatenate([q_ref[s * w:(s + 1) * w, h * hd:(h + 1) * hd] for h in range(h0, h0 + hpp)],
                                axis=0)
            sc = lax.dot_general(q, kw, (((1,), (1,)), ((), ())), preferred_element_type=F32) + bias[rows]
            if pen is not None:
                sc = sc + pen
            mx = jnp.maximum(jnp.max(sc, axis=-1, keepdims=True), sink[rows])
            p = jnp.exp2((sc - mx) * c)
            pv = jnp.dot(p.astype(BF16), vw, preferred_element_type=F32)
            denom = pv[:, hd:] + jnp.exp2((sink_rep[rows] - mx) * c)
            o = (pv[:, 0:hd] / denom).astype(o_ref.dtype)
            for h in range(hpp):
                o_ref[s * w:(s + 1) * w, (h0 + h) * hd:(h0 + h + 1) * hd] = o[h * w:(h + 1) * w]


def _attention(p, sink, q_col0, k_col0, v_col0):
    b, s, _ = p.shape
    w = WINDOW
    group = N_HEADS // N_KV_HEADS
    hd = LANES
    gw = group * hd
    tq = _tile(s, 512)
    assert s % w == 0 and tq % w == 0 and q_col0 % gw == 0 and k_col0 % hd == 0 and v_col0 % hd == 0
    r = tq // w
    nblk = s // w
    qb, kb, vb = q_col0 // gw, k_col0 // hd, v_col0 // hd
    scale = float(hd) ** -0.5

    def halo_specs(c0):
        return [
            pl.BlockSpec((None, w, hd), lambda bi, g, i: (bi, jnp.maximum(i * r - 1, 0), c0 + g)),
            pl.BlockSpec((None, tq, hd), lambda bi, g, i: (bi, i, c0 + g)),
            pl.BlockSpec((None, w, hd), lambda bi, g, i: (bi, jnp.minimum((i + 1) * r, nblk - 1), c0 + g)),
        ]

    rel = np.abs(np.arange(w)[:, None] + w - np.arange(3 * w)[None, :]).astype(np.float64)
    slopes = np.exp2(-8.0 * np.arange(1, N_HEADS + 1, dtype=np.float64) / N_HEADS)
    bias = np.where(rel[None] <= WINDOW, -slopes[:, None, None] * rel[None] / scale, NEG_BIG)
    bias = bias.reshape(N_HEADS * w, 3 * w).astype(np.float32)
    sink_rows = jnp.broadcast_to((sink.astype(F32) / scale)[:, None, None], (N_HEADS, w, LANES))
    sink_rows = sink_rows.reshape(N_HEADS * w, LANES)
    return pl.pallas_call(
        functools.partial(_attn_kernel, tq=tq, seq=s, group=group, hd=hd, scale=scale),
        grid=(b, N_KV_HEADS, s // tq),
        in_specs=[pl.BlockSpec((None, tq, gw), lambda bi, g, i: (bi, i, qb + g))]
        + halo_specs(kb) + halo_specs(vb)
        + [pl.BlockSpec((group * w, 3 * w), lambda bi, g, i: (g, 0)),
           pl.BlockSpec((group * w, LANES), lambda bi, g, i: (g, 0))],
        out_specs=pl.BlockSpec((None, tq, gw), lambda bi, g, i: (bi, i, g)),
        out_shape=jax.ShapeDtypeStruct((b, s, N_HEADS * hd), BF16),
        scratch_shapes=[pltpu.VMEM((tq + 2 * w, hd), BF16), pltpu.VMEM((tq + 2 * w, 2 * hd), BF16)],
        compiler_params=_params("parallel", "parallel", "parallel"),
        name="attention",
    )(p, p, p, p, p, p, p, jnp.asarray(bias), sink_rows)


R_FFT = BF16_SUBLANES


def _fft_factors(s):
    n2 = min(256, s // R_FFT)
    n1 = s // n2
    assert n1 * n2 == s and n1 % R_FFT == 0 and n2 % R_FFT == 0
    return n1, n2


def _fft_tables(s, gdim):
    n1, n2 = _fft_factors(s)
    r = R_FFT
    k1 = np.arange(n1).reshape(1, n1, 1, 1, 1, 1)
    jj = np.arange(r).reshape(1, 1, 1, r, 1, 1)
    nn1 = np.arange(n1).reshape(1, 1, 1, 1, n1, 1)
    jc = np.arange(r).reshape(1, 1, 1, 1, 1, r)
    jb = np.arange(n2 // r).reshape(n2 // r, 1, 1, 1, 1, 1)
    ang = 2.0 * np.pi * ((nn1 * k1 * n2 + (jb * r + jj) * k1) % s) / s
    ang = np.broadcast_to(ang, (n2 // r, n1, 1, r, n1, 1))
    eye = (jj == jc)
    t1 = np.concatenate([np.cos(ang) * eye, np.sin(ang) * eye], axis=2)
    t1 = t1.reshape(n2 // r, n1 * 2 * r, n1 * r)
    cc = np.arange(gdim)
    phi = 2.0 * np.pi * ((cc[:, None] * cc[None, :]) % gdim) / gdim
    wc = np.block([[np.cos(phi), np.sin(phi)], [-np.sin(phi), np.cos(phi)]]) / np.sqrt(gdim)
    k2 = np.arange(n2).reshape(n2, 1, 1)
    nn2 = (np.arange(n2 // r).reshape(1, n2 // r, 1) * r + np.arange(r).reshape(1, 1, r))
    beta = 2.0 * np.pi * ((k2 * nn2) % n2) / n2
    f2 = np.stack([np.cos(beta), -np.sin(beta)], axis=2).reshape(n2, 2 * n2) / np.sqrt(s)
    return (jnp.asarray(t1.astype(np.float32)).astype(BF16),
            jnp.asarray(wc.astype(np.float32)).astype(BF16),
            jnp.asarray(f2.astype(np.float32)).astype(BF16))


def _fft1_kernel(t_ref, wc_ref, u_ref, z_ref, *, gdim):
    n1, r, c = u_ref.shape
    u = u_ref[...].reshape(n1 * r, c)
    g = jnp.dot(t_ref[...], u, preferred_element_type=F32).astype(BF16).reshape(n1, 2, r, c)
    gr = g[:, 0].reshape(n1 * r, c)
    gi = g[:, 1].reshape(n1 * r, c)
    ps, qs = [], []
    for grp in range(c // gdim):
        cols = slice(grp * gdim, (grp + 1) * gdim)
        pq = jnp.dot(jnp.concatenate([gr[:, cols], gi[:, cols]], axis=1), wc_ref[...], preferred_element_type=F32)
        ps.append(pq[:, :gdim])
        qs.append(pq[:, gdim:])
    z_ref[:, 0] = jnp.concatenate(ps, axis=1).reshape(n1, r, c).astype(z_ref.dtype)
    z_ref[:, 1] = jnp.concatenate(qs, axis=1).reshape(n1, r, c).astype(z_ref.dtype)


def _fft2_kernel(f2_ref, z_ref, y_ref, scr):
    nj, rows, ct = z_ref.shape
    per_k1 = rows // R_FFT
    n2 = nj * R_FFT
    for kk in range(R_FFT):
        zin = z_ref[:, kk * per_k1:(kk + 1) * per_k1, :].reshape(2 * n2, ct)
        scr[:, kk, :] = jnp.dot(f2_ref[...], zin, preferred_element_type=F32)
    y_ref[...] = scr[...].astype(y_ref.dtype)


def _fourier(p, fwidth):
    b, s, width = p.shape
    gdim = fwidth // F_GROUPS
    assert gdim == LANES
    n1, n2 = _fft_factors(s)
    r = R_FFT
    t1, wc, f2 = _fft_tables(s, gdim)
    nj = n2 // r
    z = pl.pallas_call(
        functools.partial(_fft1_kernel, gdim=gdim),
        grid=(nj, b),
        in_specs=[
            pl.BlockSpec((None, 2 * n1 * r, n1 * r), lambda j, bi: (j, 0, 0)),
            pl.BlockSpec((2 * gdim, 2 * gdim), lambda j, bi: (0, 0)),
            pl.BlockSpec((None, n1, r, fwidth), lambda j, bi: (bi, 0, j, 0)),
        ],
        out_specs=pl.BlockSpec((None, None, n1, 2, r, fwidth), lambda j, bi: (bi, j, 0, 0, 0, 0)),
        out_shape=jax.ShapeDtypeStruct((b, nj, n1, 2, r, fwidth), BF16),
        compiler_params=_params("parallel", "parallel"),
        name="fft_stage1",
    )(t1, wc, p.reshape(b, n1, n2, width))
    ct = _tile(fwidth, 512)
    nkb = n1 // r
    z = z.reshape(b, nj, nkb, 2 * r * r, fwidth)
    y = pl.pallas_call(
        _fft2_kernel,
        grid=(b, nkb, fwidth // ct),
        in_specs=[
            pl.BlockSpec((n2, 2 * n2), lambda bi, kb, c: (0, 0)),
            pl.BlockSpec((None, nj, None, 2 * r * r, ct), lambda bi, kb, c: (bi, 0, kb, 0, c)),
        ],
        out_specs=pl.BlockSpec((None, n2, None, r, ct), lambda bi, kb, c: (bi, 0, kb, 0, c)),
        out_shape=jax.ShapeDtypeStruct((b, n2, nkb, r, fwidth), BF16),
        scratch_shapes=[pltpu.VMEM((n2, r, ct), F32)],
        compiler_params=_params("parallel", "parallel", "parallel"),
        name="fft_stage2",
    )(f2, z)
    return y.reshape(b, s, fwidth)


class _Bf16Weights:
    def __init__(self, stacked):
        self.stacked = stacked
        self.ready = {}

    def get(self, name, layer):
        key = (name, layer)
        if key not in self.ready:
            self.ready[key] = self.stacked[name][layer].astype(BF16)
        return self.ready[key]

    def jobs(self, keys):
        todo = [k for k in keys if k not in self.ready]
        return todo, [(self.stacked[name], layer) for name, layer in todo]

    def done(self, todo, arrays):
        self.ready.update(zip(todo, arrays))


def _trunk(x, mod, wts, bfw):
    b, s, d = x.shape
    m = b * s
    depth = wts["g_pre_mix"].shape[0]
    fwidth = d // 2
    qwidth = d
    kvwidth = N_KV_HEADS * (d // N_HEADS)
    q0 = fwidth
    k0 = q0 + qwidth
    v0 = k0 + kvwidth
    g0 = v0 + kvwidth
    ne, fe = wts["n_experts"], wts["expert_width"]

    def mods(l):
        return [mod[l, :, k * d:(k + 1) * d].reshape(b, 1, d) for k in range(N_MOD)]

    def branch_keys(l):
        return [("w_fourier_out", l), ("w_attn_out", l), ("w_out", l)]

    sh1, sc1, ga1, sh2, sc2, ga2 = mods(0)
    h = _prenorm(x, wts["g_pre_mix"][0], sh1, sc1)
    for l in range(depth):
        i = l // 2
        dense = l % 2 == 0
        ffn_keys = ([("w_dense_gate", i), ("w_dense_up", i)] if dense
                    else [("w_moe_gate", i), ("w_moe_up", i), ("w_moe_down", i)])
        todo, jobs = bfw.jobs(branch_keys(l) + ffn_keys)
        p, cast = _mm(h.reshape(m, d), bfw.get("w_in", l), sigmoid_from_col=g0, cast_jobs=jobs)
        bfw.done(todo, cast)
        p3 = p.reshape(b, s, -1)
        y = _fourier(p3, fwidth)
        a = _attention(p3, wts["attn_sink"][l], q0, k0, v0)
        mg = _merge(y.reshape(m, fwidth), a.reshape(m, qwidth), p, bfw.get("w_fourier_out", l),
                    bfw.get("w_attn_out", l), g0)
        more = l + 1 < depth
        todo, jobs = bfw.jobs(branch_keys(l + 1) if more else [])
        mo, cast = _mm(mg, bfw.get("w_out", l), cast_jobs=jobs)
        bfw.done(todo, cast)
        router = None if dense else (wts["w_router"][i], wts["b_router"][i])
        x, h2, *route_t = _post(x, mo, wts["g_post_mix"][l], ga1, (wts["g_pre_ffn"][l], sh2, sc2), router=router)
        gate2 = ga2
        nxt = None
        if more:
            sh1, sc1, ga1, sh2, sc2, ga2 = mods(l + 1)
            nxt = (wts["g_pre_mix"][l + 1], sh1, sc1)
        if dense:
            todo, jobs = bfw.jobs([("w_dense_down", i)] + ([("w_in", l + 1)] if more else []))
            hid, cast = _glu(h2.reshape(m, d), bfw.get("w_dense_gate", i), bfw.get("w_dense_up", i), cast_jobs=jobs)
            bfw.done(todo, cast)
            yo, _ = _mm(hid, bfw.get("w_dense_down", i))
            out = _post(x, yo, wts["g_post_ffn"][l], gate2, nxt)
        else:
            route = route_t[0].T
            rank, counts = _rank(route, ne)
            plan = _moe_plan(route, rank, counts, ne, EXPERT_TILE)
            hid = _glu_sparse(h2.reshape(m, d // 2), bfw.get("w_moe_gate", i).reshape(ne, d, fe),
                              bfw.get("w_moe_up", i).reshape(ne, d, fe), plan, EXPERT_TILE)
            ys = _down_sparse(hid, bfw.get("w_moe_down", i).reshape(ne, fe, d), plan, EXPERT_TILE)
            out = _post_moe(x, ys, route, plan, wts["g_post_ffn"][l], gate2, nxt)
        x, h = out if nxt is not None else (out, None)
    return x


def kernel(x_prompt, x_sample, c_prompt, c_sample, w_ada, b_ada, g_pre_mix, g_post_mix, w_in, w_fourier_out,
           w_attn_out, attn_sink, w_out, g_pre_ffn, g_post_ffn, w_dense_gate, w_dense_up, w_dense_down,
           w_router, b_router, w_moe_gate, w_moe_up, w_moe_down):
    bp, bs = c_prompt.shape[0], c_sample.shape[0]
    d = c_prompt.shape[1]
    rows = -(-(bp + bs) // 8) * 8
    c_all = jnp.zeros((rows, d), F32).at[:bp].set(c_prompt).at[bp:bp + bs].set(c_sample)
    mod = _adaln(c_all, w_ada, b_ada)

    n_moe, ne, _, fe = w_moe_gate.shape
    bfw = _Bf16Weights(dict(
        w_in=w_in, w_fourier_out=w_fourier_out, w_attn_out=w_attn_out, w_out=w_out,
        w_dense_gate=w_dense_gate, w_dense_up=w_dense_up, w_dense_down=w_dense_down,
        w_moe_gate=w_moe_gate.reshape(n_moe, ne * d, fe), w_moe_up=w_moe_up.reshape(n_moe, ne * d, fe),
        w_moe_down=w_moe_down.reshape(n_moe, ne * fe, d),
    ))
    wts = dict(
        w_router=w_router, b_router=b_router, attn_sink=attn_sink, n_experts=ne, expert_width=fe,
        g_pre_mix=g_pre_mix, g_post_mix=g_post_mix, g_pre_ffn=g_pre_ffn, g_post_ffn=g_post_ffn,
    )
    y_prompt = _trunk(x_prompt, mod[:, :bp], wts, bfw)
    y_sample = _trunk(x_sample, mod[:, bp:bp + bs], wts, bfw)
    return (y_prompt, y_sample)
```

```python
import functools

import numpy as np
import jax
import jax.numpy as jnp
from jax import lax
from jax.experimental import pallas as pl
from jax.experimental.pallas import tpu as pltpu

N_HEADS = 32
N_KV_HEADS = 8
WINDOW = 128
F_GROUPS = 16
N_EXPERTS = 8
N_MOD = 6
EPS = 1e-6

LANES = 128
BF16_SUBLANES = 16
VMEM_LIMIT_BYTES = 56 * 1024 * 1024

F32 = jnp.float32
BF16 = jnp.bfloat16
NEG_BIG = -1e30


def _params(*semantics):
    return pltpu.CompilerParams(dimension_semantics=semantics, vmem_limit_bytes=VMEM_LIMIT_BYTES)


def _sigmoid(x):
    return 0.5 * jnp.tanh(0.5 * x) + 0.5


def _tile(n, pref):
    t = min(n, pref)
    while n % t:
        t -= 1
    return t


def _adaln_kernel(c_ref, w_ref, b_ref, o_ref):
    c = c_ref[...]
    s = (c * _sigmoid(c)).astype(BF16)
    o_ref[...] = jnp.dot(s, w_ref[...].astype(BF16), preferred_element_type=F32) + b_ref[...]


def _adaln(c_all, w_ada, b_ada):
    depth, d, n = w_ada.shape
    r = c_all.shape[0]
    tn = _tile(n, 512)
    return pl.pallas_call(
        _adaln_kernel,
        grid=(depth, n // tn),
        in_specs=[
            pl.BlockSpec((r, d), lambda l, j: (0, 0)),
            pl.BlockSpec((None, d, tn), lambda l, j: (l, 0, j)),
            pl.BlockSpec((None, 1, tn), lambda l, j: (l, 0, j)),
        ],
        out_specs=pl.BlockSpec((None, r, tn), lambda l, j: (l, 0, j)),
        out_shape=jax.ShapeDtypeStruct((depth, r, n), F32),
        compiler_params=_params("parallel", "parallel"),
        name="adaln",
    )(c_all, w_ada, b_ada.reshape(depth, 1, n))


def _rms(x, g):
    return x * lax.rsqrt(jnp.mean(x * x, axis=-1, keepdims=True) + EPS) * g


def _prenorm_kernel(x_ref, g_ref, sh_ref, sc_ref, h_ref):
    h = _rms(x_ref[...], g_ref[...]) * (1.0 + sc_ref[...]) + sh_ref[...]
    h_ref[...] = h.astype(h_ref.dtype)


def _row_specs(ts, d):
    tok = pl.BlockSpec((None, ts, d), lambda b, i: (b, i, 0))
    vec = pl.BlockSpec((1, d), lambda b, i: (0, 0))
    per_seq = pl.BlockSpec((None, 1, d), lambda b, i: (b, 0, 0))
    return tok, vec, per_seq


def _prenorm(x, g, shift, scale):
    b, s, d = x.shape
    ts = _tile(s, 256)
    tok, vec, per_seq = _row_specs(ts, d)
    return pl.pallas_call(
        _prenorm_kernel,
        grid=(b, s // ts),
        in_specs=[tok, vec, per_seq, per_seq],
        out_specs=tok,
        out_shape=jax.ShapeDtypeStruct((b, s, d), BF16),
        compiler_params=_params("parallel", "parallel"),
        name="prenorm",
    )(x, g.reshape(1, d), shift, scale)


def _pack_halves(y):
    half = y.shape[1] // 2
    lo = lax.bitcast_convert_type(y[:, :half].astype(BF16).astype(F32), jnp.uint32)
    hi = lax.bitcast_convert_type(y[:, half:].astype(BF16).astype(F32), jnp.uint32)
    return (lo >> 16) | (hi & jnp.uint32(0xFFFF0000))


def _unpack_halves(w):
    lo = lax.bitcast_convert_type(w << 16, F32)
    hi = lax.bitcast_convert_type(w & jnp.uint32(0xFFFF0000), F32)
    return jnp.concatenate([lo, hi], axis=1)


ROUTE_LANES = 8


def _top2_route(logits):
    ne, t = logits.shape
    row = lax.broadcasted_iota(jnp.int32, logits.shape, 0)
    v1 = jnp.max(logits, axis=0, keepdims=True)
    i1 = jnp.min(jnp.where(logits == v1, row, ne), axis=0, keepdims=True)
    rest = jnp.where(row == i1, -jnp.inf, logits)
    v2 = jnp.max(rest, axis=0, keepdims=True)
    i2 = jnp.min(jnp.where(rest == v2, row, ne), axis=0, keepdims=True)
    e2 = jnp.exp(v2 - v1)
    w1 = 1.0 / (1.0 + e2)
    w2 = e2 / (1.0 + e2)
    out_row = lax.broadcasted_iota(jnp.int32, (ROUTE_LANES, t), 0)
    return jnp.where(out_row == 0, w1,
                     jnp.where(out_row == 1, w2,
                               jnp.where(out_row == 2, i1.astype(F32),
                                         jnp.where(out_row == 3, i2.astype(F32), 0.0))))


def _post_kernel(x_ref, m_ref, gp_ref, ga_ref, *rest, mode):
    xn = x_ref[...] + ga_ref[...] * _rms(m_ref[...].astype(F32), gp_ref[...])
    if mode is None:
        (xo_ref,) = rest
        xo_ref[...] = xn
        return
    gn_ref, sh_ref, sc_ref = rest[:3]
    h = _rms(xn, gn_ref[...]) * (1.0 + sc_ref[...]) + sh_ref[...]
    if mode == "next":
        xo_ref, h_ref = rest[3:]
        h_ref[...] = h.astype(h_ref.dtype)
    else:
        wt_ref, b_ref, xo_ref, h_ref, route_ref = rest[3:]
        h_ref[...] = _pack_halves(h)
        logits = lax.dot_general(wt_ref[...], h, (((1,), (1,)), ((), ())), preferred_element_type=F32,
                                 precision=lax.Precision.HIGHEST) + b_ref[...]
        route_ref[...] = _top2_route(logits)
    xo_ref[...] = xn


def _post(x, m, g_post, gate, nxt=None, router=None):
    b, s, d = x.shape
    ts = _tile(s, 256)
    per_s = s // ts
    tok, vec, per_seq = _row_specs(ts, d)
    ins = [x, m.reshape(b, s, d), g_post.reshape(1, d), gate]
    in_specs = [tok, tok, vec, per_seq]
    out_shape = [jax.ShapeDtypeStruct((b, s, d), F32)]
    out_specs = [tok]
    mode = None
    if nxt is not None:
        mode = "next"
        g_next, shift, scale = nxt
        ins += [g_next.reshape(1, d), shift, scale]
        in_specs += [vec, per_seq, per_seq]
        if router is None:
            out_shape.append(jax.ShapeDtypeStruct((b, s, d), BF16))
            out_specs.append(tok)
        else:
            mode = "route"
            w_router, b_router = router
            ne = w_router.shape[1]
            ins += [w_router.T, b_router.reshape(ne, 1)]
            in_specs += [pl.BlockSpec((ne, d), lambda bi, i: (0, 0)), pl.BlockSpec((ne, 1), lambda bi, i: (0, 0))]
            out_shape += [jax.ShapeDtypeStruct((b, s, d // 2), jnp.uint32),
                          jax.ShapeDtypeStruct((ROUTE_LANES, b * s), F32)]
            out_specs += [pl.BlockSpec((None, ts, d // 2), lambda bi, i: (bi, i, 0)),
                          pl.BlockSpec((ROUTE_LANES, ts), lambda bi, i: (0, bi * per_s + i))]
    out = pl.pallas_call(
        functools.partial(_post_kernel, mode=mode),
        grid=(b, s // ts),
        in_specs=in_specs,
        out_specs=out_specs,
        out_shape=out_shape,
        compiler_params=_params("parallel", "parallel"),
        name="post",
    )(*ins)
    return out if nxt is not None else out[0]


def _cast_job_specs(jobs, grid_ij):
    gi, gj = grid_ij
    steps = gi * gj
    ins, in_specs, out_shapes, out_specs = [], [], [], []
    for arr, layer in jobs:
        _, r, c = arr.shape
        split = None
        for cb in (1, 2, 4, 8, 16, 32):
            rb = steps // cb
            if steps % cb == 0 and r % rb == 0 and (r // rb) % BF16_SUBLANES == 0 and c % (cb * LANES) == 0:
                split = (rb, cb)
                break
        assert split is not None, (arr.shape, grid_ij)
        rb, cb = split

        def in_map(i, j, *_, layer=layer, cb=cb):
            step = i * gj + j
            return (layer, step // cb, step % cb)

        def out_map(i, j, *_, cb=cb):
            step = i * gj + j
            return (step // cb, step % cb)

        ins.append(arr)
        in_specs.append(pl.BlockSpec((None, r // rb, c // cb), in_map))
        out_shapes.append(jax.ShapeDtypeStruct((r, c), BF16))
        out_specs.append(pl.BlockSpec((r // rb, c // cb), out_map))
    return ins, in_specs, out_shapes, out_specs


def _run_cast_jobs(src_refs, dst_refs):
    for src, dst in zip(src_refs, dst_refs):
        dst[...] = src[...].astype(dst.dtype)


def _mm_kernel(a_ref, w_ref, *rest, nk, sigmoid_from, njobs):
    o_ref = rest[njobs]
    scratch = rest[2 * njobs + 1:]
    _run_cast_jobs(rest[:njobs], rest[njobs + 1:2 * njobs + 1])
    def product(cols=slice(None)):
        return jnp.dot(a_ref[...], w_ref[:, cols], preferred_element_type=F32)

    if nk > 1:
        assert sigmoid_from is None
        (acc_ref,) = scratch
        k = pl.program_id(2)

        @pl.when(k == 0)
        def _():
            acc_ref[...] = product()

        @pl.when(jnp.logical_and(k > 0, k < nk - 1))
        def _():
            acc_ref[...] += product()

        @pl.when(k == nk - 1)
        def _():
            o_ref[...] = (acc_ref[...] + product()).astype(o_ref.dtype)
    elif sigmoid_from is None:
        o_ref[...] = product().astype(o_ref.dtype)
    else:
        j = pl.program_id(1)

        @pl.when(j < sigmoid_from)
        def _():
            o_ref[...] = product().astype(o_ref.dtype)

        @pl.when(j >= sigmoid_from)
        def _():
            half = o_ref.shape[1] // 2
            for cols in (slice(0, half), slice(half, 2 * half)):
                o_ref[:, cols] = _sigmoid(product(cols)).astype(o_ref.dtype)


def _mm(a, w, *, sigmoid_from_col=None, tm=1024, tn=1024, tk=4096, cast_jobs=()):
    m, kdim = a.shape
    n = w.shape[1]
    tm, tn, tk = _tile(m, tm), _tile(n, tn), _tile(kdim, tk)
    nk = kdim // tk
    sig = None
    if sigmoid_from_col is not None:
        assert sigmoid_from_col % tn == 0
        sig = sigmoid_from_col // tn
    grid = (m // tm, n // tn, nk)
    job_ins, job_in_specs, job_shapes, job_out_specs = _cast_job_specs(cast_jobs, grid[:2])
    out = pl.pallas_call(
        functools.partial(_mm_kernel, nk=nk, sigmoid_from=sig, njobs=len(cast_jobs)),
        grid=grid,
        in_specs=[
            pl.BlockSpec((tm, tk), lambda i, j, k: (i, k)),
            pl.BlockSpec((tk, tn), lambda i, j, k: (k, j)),
        ] + job_in_specs,
        out_specs=[pl.BlockSpec((tm, tn), lambda i, j, k: (i, j))] + job_out_specs,
        out_shape=[jax.ShapeDtypeStruct((m, n), BF16)] + job_shapes,
        scratch_shapes=[pltpu.VMEM((tm, tn), F32)] if nk > 1 else [],
        compiler_params=_params("parallel", "parallel", "arbitrary"),
        name="mm",
    )(a, w, *job_ins)
    return out[0], list(out[1:])


def _merge_kernel(y_ref, a_ref, wf_ref, wa_ref, gf_ref, ga_ref, o_ref):
    f = jnp.dot(y_ref[...], wf_ref[...], preferred_element_type=F32)
    a = jnp.dot(a_ref[...], wa_ref[...], preferred_element_type=F32)
    o = gf_ref[...].astype(F32) * f + ga_ref[...].astype(F32) * a
    o_ref[...] = o.astype(o_ref.dtype)


def _merge(y, a, p, w_f, w_a, gate_col0):
    m, f = y.shape
    q = a.shape[1]
    d = w_f.shape[1]
    tm, tn = _tile(m, 1024), _tile(d, 512)
    assert gate_col0 % tn == 0
    g0 = gate_col0 // tn
    nd = d // tn
    return pl.pallas_call(
        _merge_kernel,
        grid=(m // tm, nd),
        in_specs=[
            pl.BlockSpec((tm, f), lambda i, j: (i, 0)),
            pl.BlockSpec((tm, q), lambda i, j: (i, 0)),
            pl.BlockSpec((f, tn), lambda i, j: (0, j)),
            pl.BlockSpec((q, tn), lambda i, j: (0, j)),
            pl.BlockSpec((tm, tn), lambda i, j: (i, g0 + j)),
            pl.BlockSpec((tm, tn), lambda i, j: (i, g0 + nd + j)),
        ],
        out_specs=pl.BlockSpec((tm, tn), lambda i, j: (i, j)),
        out_shape=jax.ShapeDtypeStruct((m, d), BF16),
        compiler_params=_params("parallel", "parallel"),
        name="merge",
    )(y, a, w_f, w_a, p, p)


def _glu_kernel(h_ref, wg_ref, wu_ref, *rest, njobs):
    o_ref = rest[njobs]
    _run_cast_jobs(rest[:njobs], rest[njobs + 1:])
    h = h_ref[...]
    g = jnp.dot(h, wg_ref[...], preferred_element_type=F32)
    u = jnp.dot(h, wu_ref[...], preferred_element_type=F32)
    o_ref[...] = (g * _sigmoid(g) * u).astype(o_ref.dtype)


def _glu(h, w_gate, w_up, cast_jobs=()):
    m, d = h.shape
    f = w_gate.shape[1]
    tm, tn = _tile(m, 1024), _tile(f, 512)
    wspec = pl.BlockSpec((d, tn), lambda i, j: (0, j))
    grid = (m // tm, f // tn)
    job_ins, job_in_specs, job_shapes, job_out_specs = _cast_job_specs(cast_jobs, grid)
    out = pl.pallas_call(
        functools.partial(_glu_kernel, njobs=len(cast_jobs)),
        grid=grid,
        in_specs=[pl.BlockSpec((tm, d), lambda i, j: (i, 0)), wspec, wspec] + job_in_specs,
        out_specs=[pl.BlockSpec((tm, tn), lambda i, j: (i, j))] + job_out_specs,
        out_shape=[jax.ShapeDtypeStruct((m, f), BF16)] + job_shapes,
        compiler_params=_params("parallel", "parallel"),
        name="glu",
    )(h, w_gate, w_up, *job_ins)
    return out[0], list(out[1:])


EXPERT_TILE = 256


def _rank_kernel(route_ref, tri_ref, rank_ref, cnt_ref, carry, *, ne):
    @pl.when(pl.program_id(0) == 0)
    def _():
        carry[...] = jnp.zeros_like(carry)

    r = route_ref[...]
    lane = lax.broadcasted_iota(jnp.int32, (r.shape[0], 2 * ne), 1)
    sel = jnp.where(lane < ne, r[:, 2:3], r[:, 3:4] + float(ne))
    onehot = lane.astype(F32) == sel
    oh = jnp.where(onehot, 1.0, 0.0)
    before = jnp.dot(tri_ref[...], oh.astype(BF16), preferred_element_type=F32) + carry[...]
    ranks = jnp.where(onehot, before, 0.0)
    r0 = jnp.sum(jnp.where(lane < ne, ranks, 0.0), axis=-1, keepdims=True)
    r1 = jnp.sum(jnp.where(lane >= ne, ranks, 0.0), axis=-1, keepdims=True)
    out_lane = lax.broadcasted_iota(jnp.int32, rank_ref.shape, 1)
    rank_ref[...] = jnp.where(out_lane == 0, r0, jnp.where(out_lane == 1, r1, 0.0))
    carry[...] += jnp.sum(oh, axis=0, keepdims=True)
    cnt_ref[...] = carry[...]


def _rank(route, ne):
    m = route.shape[0]
    tr = _tile(m, 512)
    tri = np.tril(np.ones((tr, tr), np.float32), -1)
    return pl.pallas_call(
        functools.partial(_rank_kernel, ne=ne),
        grid=(m // tr,),
        in_specs=[pl.BlockSpec((tr, ROUTE_LANES), lambda i: (i, 0)),
                  pl.BlockSpec((tr, tr), lambda i: (0, 0))],
        out_specs=[pl.BlockSpec((tr, ROUTE_LANES), lambda i: (i, 0)),
                   pl.BlockSpec((1, 2 * ne), lambda i: (0, 0))],
        out_shape=[jax.ShapeDtypeStruct((m, ROUTE_LANES), F32),
                   jax.ShapeDtypeStruct((1, 2 * ne), F32)],
        scratch_shapes=[pltpu.VMEM((1, 2 * ne), F32)],
        compiler_params=_params("arbitrary"),
        name="rank",
    )(route, jnp.asarray(tri).astype(BF16))


def _moe_plan(route, rank, counts, ne, tm):
    m = route.shape[0]
    nt = (2 * m) // tm + ne
    c0 = counts[0, :ne].astype(jnp.int32)
    c = c0 + counts[0, ne:].astype(jnp.int32)
    padded = ((c + tm - 1) // tm) * tm
    ends = jnp.cumsum(padded)
    off = ends - padded
    i1 = route[:, 2].astype(jnp.int32)
    i2 = route[:, 3].astype(jnp.int32)
    pos0 = off[i1] + rank[:, 0].astype(jnp.int32)
    pos1 = off[i2] + c0[i2] + rank[:, 1].astype(jnp.int32)
    tok = jnp.arange(m, dtype=jnp.int32)
    src = jnp.zeros((nt * tm,), jnp.int32).at[pos0].set(tok).at[pos1].set(tok)
    tile_expert = jnp.sum(jnp.arange(nt, dtype=jnp.int32)[:, None] * tm >= ends[None, :], axis=1)
    tile_expert = jnp.minimum(tile_expert, ne - 1).astype(jnp.int32)
    n_used = (ends[-1] // tm).astype(jnp.int32).reshape(1)
    return dict(src=src, pos=jnp.concatenate([pos0, pos1]), tile_expert=tile_expert, n_used=n_used, nt=nt)


GATHER_DEPTH = 3


def _glu_sparse_kernel(src_ref, te_ref, nu_ref, h_hbm, wg_ref, wu_ref, o_ref, *scratch, tm):
    bufs, sem = scratch[:GATHER_DEPTH], scratch[GATHER_DEPTH]
    ahead = GATHER_DEPTH - 1
    t = pl.program_id(0)
    nt = pl.num_programs(0)
    n_used = nu_ref[0]

    def row_copy(row, sl, r):
        return pltpu.make_async_copy(h_hbm.at[pl.ds(row, 1)], bufs[sl].at[pl.ds(r, 1)], sem.at[sl])

    def wait_tile(sl):
        def body(r, carry):
            row_copy(0, sl, 0).wait()
            return carry
        lax.fori_loop(0, tm, body, 0, unroll=8)

    @pl.when(t == 0)
    def _():
        for sl in range(ahead):
            def body(r, carry, sl=sl):
                row_copy(src_ref[sl * tm + r], sl, r).start()
                return carry
            lax.fori_loop(0, tm, body, 0, unroll=8)

    def used_step(sl):
        wait_tile(sl)
        nxt = jnp.minimum(t + ahead, nt - 1) * tm
        dst = (sl + ahead) % GATHER_DEPTH
        x = _unpack_halves(bufs[sl][...]).astype(BF16)
        for r in range(tm):
            row_copy(src_ref[nxt + r], dst, r).start(priority=r % 2)
        g = jnp.dot(x, wg_ref[...], preferred_element_type=F32)
        u = jnp.dot(x, wu_ref[...], preferred_element_type=F32)
        o_ref[...] = (g * _sigmoid(g) * u).astype(o_ref.dtype)

    for sl in range(GATHER_DEPTH):
        @pl.when(jnp.logical_and(t < n_used, t % GATHER_DEPTH == sl))
        def _(sl=sl):
            used_step(sl)

        @pl.when(jnp.logical_and(t == n_used - 1, t % GATHER_DEPTH == sl))
        def _(sl=sl):
            for k in range(1, GATHER_DEPTH):
                wait_tile((sl + k) % GATHER_DEPTH)

    @pl.when(t >= n_used)
    def _():
        o_ref[...] = jnp.zeros_like(o_ref)


def _glu_sparse(h, w_gate, w_up, plan, tm):
    _, d, fe = w_gate.shape
    nt = plan["nt"]
    wspec = pl.BlockSpec((None, d, fe), lambda t, src, te, nu: (te[t], 0, 0))
    return pl.pallas_call(
        functools.partial(_glu_sparse_kernel, tm=tm),
        grid_spec=pltpu.PrefetchScalarGridSpec(
            num_scalar_prefetch=3,
            grid=(nt,),
            in_specs=[pl.BlockSpec(memory_space=pl.ANY), wspec, wspec],
            out_specs=pl.BlockSpec((tm, fe), lambda t, src, te, nu: (t, 0)),
            scratch_shapes=[pltpu.VMEM((tm, d // 2), jnp.uint32)] * GATHER_DEPTH
            + [pltpu.SemaphoreType.DMA((GATHER_DEPTH,))],
        ),
        out_shape=jax.ShapeDtypeStruct((nt * tm, fe), BF16),
        compiler_params=_params("arbitrary"),
        name="glu_sparse",
    )(plan["src"], plan["tile_expert"], plan["n_used"], h, w_gate, w_up)


def _down_sparse_kernel(te_ref, nu_ref, hid_ref, w_ref, o_ref):
    t = pl.program_id(0)

    @pl.when(t < nu_ref[0])
    def _():
        o_ref[...] = _pack_halves(jnp.dot(hid_ref[...], w_ref[...], preferred_element_type=F32))

    @pl.when(t >= nu_ref[0])
    def _():
        o_ref[...] = jnp.zeros_like(o_ref)


def _down_sparse(hid, w_down, plan, tm):
    _, fe, d = w_down.shape
    nt = plan["nt"]
    return pl.pallas_call(
        _down_sparse_kernel,
        grid_spec=pltpu.PrefetchScalarGridSpec(
            num_scalar_prefetch=2,
            grid=(nt,),
            in_specs=[pl.BlockSpec((tm, fe), lambda t, te, nu: (t, 0)),
                      pl.BlockSpec((None, fe, d), lambda t, te, nu: (te[t], 0, 0))],
            out_specs=pl.BlockSpec((tm, d // 2), lambda t, te, nu: (t, 0)),
        ),
        out_shape=jax.ShapeDtypeStruct((nt * tm, d // 2), jnp.uint32),
        compiler_params=_params("arbitrary"),
        name="down_sparse",
    )(plan["tile_expert"], plan["n_used"], hid, w_down)


def _post_moe_kernel(pos_ref, x_ref, route_ref, gp_ref, ga_ref, ys_hbm, *rest, ts, m_total, with_next):
    if with_next:
        gn_ref, sh_ref, sc_ref, xo_ref, h_ref, buf0, buf1, sem = rest
    else:
        xo_ref, buf0, buf1, sem = rest
    i = pl.program_id(0)
    last = pl.num_programs(0) - 1
    bufs = (buf0, buf1)

    def row_copy(row, sl, k, r):
        return pltpu.make_async_copy(ys_hbm.at[pl.ds(row, 1)], bufs[sl].at[k, pl.ds(r, 1)], sem.at[sl])

    def wait_tile(sl):
        def body(r, carry):
            row_copy(0, sl, 0, 0).wait()
            return carry
        lax.fori_loop(0, 2 * ts, body, 0, unroll=8)

    @pl.when(i == 0)
    def _():
        def body(r, carry):
            for k in range(2):
                row_copy(pos_ref[k * m_total + r], 0, k, r).start()
            return carry
        lax.fori_loop(0, ts, body, 0, unroll=4)

    def step(sl):
        wait_tile(sl)
        nxt = jnp.minimum(i + 1, last) * ts
        route = route_ref[...]
        y = (route[:, 0:1] * _unpack_halves(bufs[sl][0]) + route[:, 1:2] * _unpack_halves(bufs[sl][1]))
        for r in range(ts):
            for k in range(2):
                row_copy(pos_ref[k * m_total + nxt + r], 1 - sl, k, r).start()
        xn = x_ref[...] + ga_ref[...] * _rms(y, gp_ref[...])
        xo_ref[...] = xn
        if with_next:
            h = _rms(xn, gn_ref[...]) * (1.0 + sc_ref[...]) + sh_ref[...]
            h_ref[...] = h.astype(h_ref.dtype)

    for sl in range(2):
        @pl.when(i % 2 == sl)
        def _(sl=sl):
            step(sl)

        @pl.when(jnp.logical_and(i == last, i % 2 == sl))
        def _(sl=sl):
            wait_tile(1 - sl)


def _post_moe(x, ys, route, plan, g_post, gate, nxt=None):
    b, s, d = x.shape
    m = b * s
    ts = _tile(s, 256)
    per_s = s // ts
    tok = pl.BlockSpec((ts, d), lambda i, pos: (i, 0))
    vec = pl.BlockSpec((1, d), lambda i, pos: (0, 0))
    per_seq = pl.BlockSpec((None, 1, d), lambda i, pos: (i // per_s, 0, 0))
    ins = [x.reshape(m, d), route, g_post.reshape(1, d), gate, ys]
    in_specs = [tok, pl.BlockSpec((ts, ROUTE_LANES), lambda i, pos: (i, 0)), vec, per_seq,
                pl.BlockSpec(memory_space=pl.ANY)]
    out_shape = [jax.ShapeDtypeStruct((m, d), F32)]
    out_specs = [tok]
    if nxt is not None:
        g_next, shift, scale = nxt
        ins += [g_next.reshape(1, d), shift, scale]
        in_specs += [vec, per_seq, per_seq]
        out_shape.append(jax.ShapeDtypeStruct((m, d), BF16))
        out_specs.append(tok)
    out = pl.pallas_call(
        functools.partial(_post_moe_kernel, ts=ts, m_total=m, with_next=nxt is not None),
        grid_spec=pltpu.PrefetchScalarGridSpec(
            num_scalar_prefetch=1,
            grid=(m // ts,),
            in_specs=in_specs,
            out_specs=out_specs,
            scratch_shapes=[pltpu.VMEM((2, ts, d // 2), jnp.uint32), pltpu.VMEM((2, ts, d // 2), jnp.uint32),
                            pltpu.SemaphoreType.DMA((2,))],
        ),
        out_shape=out_shape,
        compiler_params=_params("arbitrary"),
        name="post_moe",
    )(plan["pos"], *ins)
    if nxt is not None:
        return out[0].reshape(b, s, d), out[1].reshape(b, s, d)
    return out[0].reshape(b, s, d)


LOG2E = 1.4426950408889634
ATTN_HEADS_PER_PASS = 4


def _attn_kernel(q_ref, kp_ref, km_ref, kn_ref, vp_ref, vm_ref, vn_ref, bias_ref, sink_ref, o_ref,
                 k_scr, v_scr, *, tq, seq, group, hd, scale):
    w = WINDOW
    i = pl.program_id(2)
    k_scr[0:w] = kp_ref[...]
    k_scr[w:w + tq] = km_ref[...]
    k_scr[w + tq:] = kn_ref[...]
    v_scr[0:w, 0:hd] = vp_ref[...]
    v_scr[w:w + tq, 0:hd] = vm_ref[...]
    v_scr[w + tq:, 0:hd] = vn_ref[...]
    v_scr[:, hd:] = jnp.ones((tq + 2 * w, hd), v_scr.dtype)
    c = scale * LOG2E
    bias = bias_ref[...]
    sink_rep = sink_ref[...]
    sink = sink_rep[:, 0:1]
    nsub = tq // w
    hpp = ATTN_HEADS_PER_PASS
    for s in range(nsub):
        kw = k_scr[s * w:(s + 3) * w]
        vw = v_scr[s * w:(s + 3) * w]
        pen = None
        if s == 0 or s == nsub - 1:
            key_pos = lax.broadcasted_iota(jnp.int32, (1, 3 * w), 1) + (i * tq + (s - 1) * w)
            valid = jnp.logical_and(key_pos >= 0, key_pos < seq)
            pen = jnp.where(valid, 0.0, NEG_BIG)
        for h0 in range(0, group, hpp):
            rows = slice(h0 * w, (h0 + hpp) * w)
            q = jnp.concatenate([q_ref[s * w:(s + 1) * w, h * hd:(h + 1) * hd] for h in range(h0, h0 + hpp)],
                                axis=0)
            sc = lax.dot_general(q, kw, (((1,), (1,)), ((), ())), preferred_element_type=F32) + bias[rows]
            if pen is not None:
                sc = sc + pen
            mx = jnp.maximum(jnp.max(sc, axis=-1, keepdims=True), sink[rows])
            p = jnp.exp2((sc - mx) * c)
            pv = jnp.dot(p.astype(BF16), vw, preferred_element_type=F32)
            denom = pv[:, hd:] + jnp.exp2((sink_rep[rows] - mx) * c)
            o = (pv[:, 0:hd] / denom).astype(o_ref.dtype)
            for h in range(hpp):
                o_ref[s * w:(s + 1) * w, (h0 + h) * hd:(h0 + h + 1) * hd] = o[h * w:(h + 1) * w]


def _attention(p, sink, q_col0, k_col0, v_col0):
    b, s, _ = p.shape
    w = WINDOW
    group = N_HEADS // N_KV_HEADS
    hd = LANES
    gw = group * hd
    tq = _tile(s, 512)
    assert s % w == 0 and tq % w == 0 and q_col0 % gw == 0 and k_col0 % hd == 0 and v_col0 % hd == 0
    r = tq // w
    nblk = s // w
    qb, kb, vb = q_col0 // gw, k_col0 // hd, v_col0 // hd
    scale = float(hd) ** -0.5

    def halo_specs(c0):
        return [
            pl.BlockSpec((None, w, hd), lambda bi, g, i: (bi, jnp.maximum(i * r - 1, 0), c0 + g)),
            pl.BlockSpec((None, tq, hd), lambda bi, g, i: (bi, i, c0 + g)),
            pl.BlockSpec((None, w, hd), lambda bi, g, i: (bi, jnp.minimum((i + 1) * r, nblk - 1), c0 + g)),
        ]

    rel = np.abs(np.arange(w)[:, None] + w - np.arange(3 * w)[None, :]).astype(np.float64)
    slopes = np.exp2(-8.0 * np.arange(1, N_HEADS + 1, dtype=np.float64) / N_HEADS)
    bias = np.where(rel[None] <= WINDOW, -slopes[:, None, None] * rel[None] / scale, NEG_BIG)
    bias = bias.reshape(N_HEADS * w, 3 * w).astype(np.float32)
    sink_rows = jnp.broadcast_to((sink.astype(F32) / scale)[:, None, None], (N_HEADS, w, LANES))
    sink_rows = sink_rows.reshape(N_HEADS * w, LANES)
    return pl.pallas_call(
        functools.partial(_attn_kernel, tq=tq, seq=s, group=group, hd=hd, scale=scale),
        grid=(b, N_KV_HEADS, s // tq),
        in_specs=[pl.BlockSpec((None, tq, gw), lambda bi, g, i: (bi, i, qb + g))]
        + halo_specs(kb) + halo_specs(vb)
        + [pl.BlockSpec((group * w, 3 * w), lambda bi, g, i: (g, 0)),
           pl.BlockSpec((group * w, LANES), lambda bi, g, i: (g, 0))],
        out_specs=pl.BlockSpec((None, tq, gw), lambda bi, g, i: (bi, i, g)),
        out_shape=jax.ShapeDtypeStruct((b, s, N_HEADS * hd), BF16),
        scratch_shapes=[pltpu.VMEM((tq + 2 * w, hd), BF16), pltpu.VMEM((tq + 2 * w, 2 * hd), BF16)],
        compiler_params=_params("parallel", "parallel", "parallel"),
        name="attention",
    )(p, p, p, p, p, p, p, jnp.asarray(bias), sink_rows)


R_FFT = BF16_SUBLANES


def _fft_factors(s):
    n2 = min(256, s // R_FFT)
    n1 = s // n2
    assert n1 * n2 == s and n1 % R_FFT == 0 and n2 % R_FFT == 0
    return n1, n2


def _fft_tables(s, gdim):
    n1, n2 = _fft_factors(s)
    r = R_FFT
    k1 = np.arange(n1).reshape(1, n1, 1, 1, 1, 1)
    jj = np.arange(r).reshape(1, 1, 1, r, 1, 1)
    nn1 = np.arange(n1).reshape(1, 1, 1, 1, n1, 1)
    jc = np.arange(r).reshape(1, 1, 1, 1, 1, r)
    jb = np.arange(n2 // r).reshape(n2 // r, 1, 1, 1, 1, 1)
    ang = 2.0 * np.pi * ((nn1 * k1 * n2 + (jb * r + jj) * k1) % s) / s
    ang = np.broadcast_to(ang, (n2 // r, n1, 1, r, n1, 1))
    eye = (jj == jc)
    t1 = np.concatenate([np.cos(ang) * eye, np.sin(ang) * eye], axis=2)
    t1 = t1.reshape(n2 // r, n1 * 2 * r, n1 * r)
    cc = np.arange(gdim)
    phi = 2.0 * np.pi * ((cc[:, None] * cc[None, :]) % gdim) / gdim
    wc = np.block([[np.cos(phi), np.sin(phi)], [-np.sin(phi), np.cos(phi)]]) / np.sqrt(gdim)
    k2 = np.arange(n2).reshape(n2, 1, 1)
    nn2 = (np.arange(n2 // r).reshape(1, n2 // r, 1) * r + np.arange(r).reshape(1, 1, r))
    beta = 2.0 * np.pi * ((k2 * nn2) % n2) / n2
    f2 = np.stack([np.cos(beta), -np.sin(beta)], axis=2).reshape(n2, 2 * n2) / np.sqrt(s)
    return (jnp.asarray(t1.astype(np.float32)).astype(BF16),
            jnp.asarray(wc.astype(np.float32)).astype(BF16),
            jnp.asarray(f2.astype(np.float32)).astype(BF16))


def _fft1_kernel(t_ref, wc_ref, u_ref, z_ref, *, gdim):
    n1, r, c = u_ref.shape
    u = u_ref[...].reshape(n1 * r, c)
    g = jnp.dot(t_ref[...], u, preferred_element_type=F32).astype(BF16).reshape(n1, 2, r, c)
    gr = g[:, 0].reshape(n1 * r, c)
    gi = g[:, 1].reshape(n1 * r, c)
    ps, qs = [], []
    for grp in range(c // gdim):
        cols = slice(grp * gdim, (grp + 1) * gdim)
        pq = jnp.dot(jnp.concatenate([gr[:, cols], gi[:, cols]], axis=1), wc_ref[...], preferred_element_type=F32)
        ps.append(pq[:, :gdim])
        qs.append(pq[:, gdim:])
    z_ref[:, 0] = jnp.concatenate(ps, axis=1).reshape(n1, r, c).astype(z_ref.dtype)
    z_ref[:, 1] = jnp.concatenate(qs, axis=1).reshape(n1, r, c).astype(z_ref.dtype)


def _fft2_kernel(f2_ref, z_ref, y_ref, scr):
    nj, rows, ct = z_ref.shape
    per_k1 = rows // R_FFT
    n2 = nj * R_FFT
    for kk in range(R_FFT):
        zin = z_ref[:, kk * per_k1:(kk + 1) * per_k1, :].reshape(2 * n2, ct)
        scr[:, kk, :] = jnp.dot(f2_ref[...], zin, preferred_element_type=F32)
    y_ref[...] = scr[...].astype(y_ref.dtype)


def _fourier(p, fwidth):
    b, s, width = p.shape
    gdim = fwidth // F_GROUPS
    assert gdim == LANES
    n1, n2 = _fft_factors(s)
    r = R_FFT
    t1, wc, f2 = _fft_tables(s, gdim)
    nj = n2 // r
    z = pl.pallas_call(
        functools.partial(_fft1_kernel, gdim=gdim),
        grid=(nj, b),
        in_specs=[
            pl.BlockSpec((None, 2 * n1 * r, n1 * r), lambda j, bi: (j, 0, 0)),
            pl.BlockSpec((2 * gdim, 2 * gdim), lambda j, bi: (0, 0)),
            pl.BlockSpec((None, n1, r, fwidth), lambda j, bi: (bi, 0, j, 0)),
        ],
        out_specs=pl.BlockSpec((None, None, n1, 2, r, fwidth), lambda j, bi: (bi, j, 0, 0, 0, 0)),
        out_shape=jax.ShapeDtypeStruct((b, nj, n1, 2, r, fwidth), BF16),
        compiler_params=_params("parallel", "parallel"),
        name="fft_stage1",
    )(t1, wc, p.reshape(b, n1, n2, width))
    ct = _tile(fwidth, 512)
    nkb = n1 // r
    z = z.reshape(b, nj, nkb, 2 * r * r, fwidth)
    y = pl.pallas_call(
        _fft2_kernel,
        grid=(b, nkb, fwidth // ct),
        in_specs=[
            pl.BlockSpec((n2, 2 * n2), lambda bi, kb, c: (0, 0)),
            pl.BlockSpec((None, nj, None, 2 * r * r, ct), lambda bi, kb, c: (bi, 0, kb, 0, c)),
        ],
        out_specs=pl.BlockSpec((None, n2, None, r, ct), lambda bi, kb, c: (bi, 0, kb, 0, c)),
        out_shape=jax.ShapeDtypeStruct((b, n2, nkb, r, fwidth), BF16),
        scratch_shapes=[pltpu.VMEM((n2, r, ct), F32)],
        compiler_params=_params("parallel", "parallel", "parallel"),
        name="fft_stage2",
    )(f2, z)
    return y.reshape(b, s, fwidth)


class _Bf16Weights:
    def __init__(self, stacked):
        self.stacked = stacked
        self.ready = {}

    def get(self, name, layer):
        key = (name, layer)
        if key not in self.ready:
            self.ready[key] = self.stacked[name][layer].astype(BF16)
        return self.ready[key]

    def jobs(self, keys):
        todo = [k for k in keys if k not in self.ready]
        return todo, [(self.stacked[name], layer) for name, layer in todo]

    def done(self, todo, arrays):
        self.ready.update(zip(todo, arrays))


def _trunk(x, mod, wts, bfw):
    b, s, d = x.shape
    m = b * s
    depth = wts["g_pre_mix"].shape[0]
    fwidth = d // 2
    qwidth = d
    kvwidth = N_KV_HEADS * (d // N_HEADS)
    q0 = fwidth
    k0 = q0 + qwidth
    v0 = k0 + kvwidth
    g0 = v0 + kvwidth
    ne, fe = wts["n_experts"], wts["expert_width"]

    def mods(l):
        return [mod[l, :, k * d:(k + 1) * d].reshape(b, 1, d) for k in range(N_MOD)]

    def branch_keys(l):
        return [("w_fourier_out", l), ("w_attn_out", l), ("w_out", l)]

    sh1, sc1, ga1, sh2, sc2, ga2 = mods(0)
    h = _prenorm(x, wts["g_pre_mix"][0], sh1, sc1)
    for l in range(depth):
        i = l // 2
        dense = l % 2 == 0
        ffn_keys = ([("w_dense_gate", i), ("w_dense_up", i)] if dense
                    else [("w_moe_gate", i), ("w_moe_up", i), ("w_moe_down", i)])
        todo, jobs = bfw.jobs(branch_keys(l) + ffn_keys)
        p, cast = _mm(h.reshape(m, d), bfw.get("w_in", l), sigmoid_from_col=g0, cast_jobs=jobs)
        bfw.done(todo, cast)
        p3 = p.reshape(b, s, -1)
        y = _fourier(p3, fwidth)
        a = _attention(p3, wts["attn_sink"][l], q0, k0, v0)
        mg = _merge(y.reshape(m, fwidth), a.reshape(m, qwidth), p, bfw.get("w_fourier_out", l),
                    bfw.get("w_attn_out", l), g0)
        more = l + 1 < depth
        todo, jobs = bfw.jobs(branch_keys(l + 1) if more else [])
        mo, cast = _mm(mg, bfw.get("w_out", l), cast_jobs=jobs)
        bfw.done(todo, cast)
        router = None if dense else (wts["w_router"][i], wts["b_router"][i])
        x, h2, *route_t = _post(x, mo, wts["g_post_mix"][l], ga1, (wts["g_pre_ffn"][l], sh2, sc2), router=router)
        gate2 = ga2
        nxt = None
        if more:
            sh1, sc1, ga1, sh2, sc2, ga2 = mods(l + 1)
            nxt = (wts["g_pre_mix"][l + 1], sh1, sc1)
        if dense:
            todo, jobs = bfw.jobs([("w_dense_down", i)] + ([("w_in", l + 1)] if more else []))
            hid, cast = _glu(h2.reshape(m, d), bfw.get("w_dense_gate", i), bfw.get("w_dense_up", i), cast_jobs=jobs)
            bfw.done(todo, cast)
            yo, _ = _mm(hid, bfw.get("w_dense_down", i))
            out = _post(x, yo, wts["g_post_ffn"][l], gate2, nxt)
        else:
            route = route_t[0].T
            rank, counts = _rank(route, ne)
            plan = _moe_plan(route, rank, counts, ne, EXPERT_TILE)
            hid = _glu_sparse(h2.reshape(m, d // 2), bfw.get("w_moe_gate", i).reshape(ne, d, fe),
                              bfw.get("w_moe_up", i).reshape(ne, d, fe), plan, EXPERT_TILE)
            ys = _down_sparse(hid, bfw.get("w_moe_down", i).reshape(ne, fe, d), plan, EXPERT_TILE)
            out = _post_moe(x, ys, route, plan, wts["g_post_ffn"][l], gate2, nxt)
        x, h = out if nxt is not None else (out, None)
    return x


def kernel(x_prompt, x_sample, c_prompt, c_sample, w_ada, b_ada, g_pre_mix, g_post_mix, w_in, w_fourier_out,
           w_attn_out, attn_sink, w_out, g_pre_ffn, g_post_ffn, w_dense_gate, w_dense_up, w_dense_down,
           w_router, b_router, w_moe_gate, w_moe_up, w_moe_down):
    bp, bs = c_prompt.shape[0], c_sample.shape[0]
    d = c_prompt.shape[1]
    rows = -(-(bp + bs) // 8) * 8
    c_all = jnp.zeros((rows, d), F32).at[:bp].set(c_prompt).at[bp:bp + bs].set(c_sample)
    mod = _adaln(c_all, w_ada, b_ada)

    n_moe, ne, _, fe = w_moe_gate.shape
    bfw = _Bf16Weights(dict(
        w_in=w_in, w_fourier_out=w_fourier_out, w_attn_out=w_attn_out, w_out=w_out,
        w_dense_gate=w_dense_gate, w_dense_up=w_dense_up, w_dense_down=w_dense_down,
        w_moe_gate=w_moe_gate.reshape(n_moe, ne * d, fe), w_moe_up=w_moe_up.reshape(n_moe, ne * d, fe),
        w_moe_down=w_moe_down.reshape(n_moe, ne * fe, d),
    ))
    wts = dict(
        w_router=w_router, b_router=b_router, attn_sink=attn_sink, n_experts=ne, expert_width=fe,
        g_pre_mix=g_pre_mix, g_post_mix=g_post_mix, g_pre_ffn=g_pre_ffn, g_post_ffn=g_post_ffn,
    )
    y_prompt = _trunk(x_prompt, mod[:, :bp], wts, bfw)
    y_sample = _trunk(x_sample, mod[:, bp:bp + bs], wts, bfw)
    return (y_prompt, y_sample)
```

```python
import functools

import numpy as np
import jax
import jax.numpy as jnp
from jax import lax
from jax.experimental import pallas as pl
from jax.experimental.pallas import tpu as pltpu

N_HEADS = 32
N_KV_HEADS = 8
WINDOW = 128
F_GROUPS = 16
N_EXPERTS = 8
N_MOD = 6
EPS = 1e-6

LANES = 128
BF16_SUBLANES = 16
VMEM_LIMIT_BYTES = 56 * 1024 * 1024

F32 = jnp.float32
BF16 = jnp.bfloat16
NEG_BIG = -1e30


def _params(*semantics):
    return pltpu.CompilerParams(dimension_semantics=semantics, vmem_limit_bytes=VMEM_LIMIT_BYTES)


def _sigmoid(x):
    return 0.5 * jnp.tanh(0.5 * x) + 0.5


def _tile(n, pref):
    t = min(n, pref)
    while n % t:
        t -= 1
    return t


def _adaln_kernel(c_ref, w_ref, b_ref, o_ref):
    c = c_ref[...]
    s = (c * _sigmoid(c)).astype(BF16)
    o_ref[...] = jnp.dot(s, w_ref[...].astype(BF16), preferred_element_type=F32) + b_ref[...]


def _adaln(c_all, w_ada, b_ada):
    depth, d, n = w_ada.shape
    r = c_all.shape[0]
    tn = _tile(n, 512)
    return pl.pallas_call(
        _adaln_kernel,
        grid=(depth, n // tn),
        in_specs=[
            pl.BlockSpec((r, d), lambda l, j: (0, 0)),
            pl.BlockSpec((None, d, tn), lambda l, j: (l, 0, j)),
            pl.BlockSpec((None, 1, tn), lambda l, j: (l, 0, j)),
        ],
        out_specs=pl.BlockSpec((None, r, tn), lambda l, j: (l, 0, j)),
        out_shape=jax.ShapeDtypeStruct((depth, r, n), F32),
        compiler_params=_params("parallel", "parallel"),
        name="adaln",
    )(c_all, w_ada, b_ada.reshape(depth, 1, n))


def _rms(x, g):
    return x * lax.rsqrt(jnp.mean(x * x, axis=-1, keepdims=True) + EPS) * g


def _prenorm_kernel(x_ref, g_ref, sh_ref, sc_ref, h_ref):
    h = _rms(x_ref[...], g_ref[...]) * (1.0 + sc_ref[...]) + sh_ref[...]
    h_ref[...] = h.astype(h_ref.dtype)


def _row_specs(ts, d):
    tok = pl.BlockSpec((None, ts, d), lambda b, i: (b, i, 0))
    vec = pl.BlockSpec((1, d), lambda b, i: (0, 0))
    per_seq = pl.BlockSpec((None, 1, d), lambda b, i: (b, 0, 0))
    return tok, vec, per_seq


def _prenorm(x, g, shift, scale):
    b, s, d = x.shape
    ts = _tile(s, 256)
    tok, vec, per_seq = _row_specs(ts, d)
    return pl.pallas_call(
        _prenorm_kernel,
        grid=(b, s // ts),
        in_specs=[tok, vec, per_seq, per_seq],
        out_specs=tok,
        out_shape=jax.ShapeDtypeStruct((b, s, d), BF16),
        compiler_params=_params("parallel", "parallel"),
        name="prenorm",
    )(x, g.reshape(1, d), shift, scale)


def _pack_halves(y):
    half = y.shape[1] // 2
    lo = lax.bitcast_convert_type(y[:, :half].astype(BF16).astype(F32), jnp.uint32)
    hi = lax.bitcast_convert_type(y[:, half:].astype(BF16).astype(F32), jnp.uint32)
    return (lo >> 16) | (hi & jnp.uint32(0xFFFF0000))


def _unpack_halves(w):
    lo = lax.bitcast_convert_type(w << 16, F32)
    hi = lax.bitcast_convert_type(w & jnp.uint32(0xFFFF0000), F32)
    return jnp.concatenate([lo, hi], axis=1)


ROUTE_LANES = 8


def _top2_route(logits):
    ne, t = logits.shape
    row = lax.broadcasted_iota(jnp.int32, logits.shape, 0)
    v1 = jnp.max(logits, axis=0, keepdims=True)
    i1 = jnp.min(jnp.where(logits == v1, row, ne), axis=0, keepdims=True)
    rest = jnp.where(row == i1, -jnp.inf, logits)
    v2 = jnp.max(rest, axis=0, keepdims=True)
    i2 = jnp.min(jnp.where(rest == v2, row, ne), axis=0, keepdims=True)
    e2 = jnp.exp(v2 - v1)
    w1 = 1.0 / (1.0 + e2)
    w2 = e2 / (1.0 + e2)
    out_row = lax.broadcasted_iota(jnp.int32, (ROUTE_LANES, t), 0)
    return jnp.where(out_row == 0, w1,
                     jnp.where(out_row == 1, w2,
                               jnp.where(out_row == 2, i1.astype(F32),
                                         jnp.where(out_row == 3, i2.astype(F32), 0.0))))


def _post_kernel(x_ref, m_ref, gp_ref, ga_ref, *rest, mode):
    xn = x_ref[...] + ga_ref[...] * _rms(m_ref[...].astype(F32), gp_ref[...])
    if mode is None:
        (xo_ref,) = rest
        xo_ref[...] = xn
        return
    gn_ref, sh_ref, sc_ref = rest[:3]
    h = _rms(xn, gn_ref[...]) * (1.0 + sc_ref[...]) + sh_ref[...]
    if mode == "next":
        xo_ref, h_ref = rest[3:]
        h_ref[...] = h.astype(h_ref.dtype)
    else:
        wt_ref, b_ref, xo_ref, h_ref, route_ref = rest[3:]
        h_ref[...] = _pack_halves(h)
        logits = lax.dot_general(wt_ref[...], h, (((1,), (1,)), ((), ())), preferred_element_type=F32,
                                 precision=lax.Precision.HIGHEST) + b_ref[...]
        route_ref[...] = _top2_route(logits)
    xo_ref[...] = xn


def _post(x, m, g_post, gate, nxt=None, router=None):
    b, s, d = x.shape
    ts = _tile(s, 256)
    per_s = s // ts
    tok, vec, per_seq = _row_specs(ts, d)
    ins = [x, m.reshape(b, s, d), g_post.reshape(1, d), gate]
    in_specs = [tok, tok, vec, per_seq]
    out_shape = [jax.ShapeDtypeStruct((b, s, d), F32)]
    out_specs = [tok]
    mode = None
    if nxt is not None:
        mode = "next"
        g_next, shift, scale = nxt
        ins += [g_next.reshape(1, d), shift, scale]
        in_specs += [vec, per_seq, per_seq]
        if router is None:
            out_shape.append(jax.ShapeDtypeStruct((b, s, d), BF16))
            out_specs.append(tok)
        else:
            mode = "route"
            w_router, b_router = router
            ne = w_router.shape[1]
            ins += [w_router.T, b_router.reshape(ne, 1)]
            in_specs += [pl.BlockSpec((ne, d), lambda bi, i: (0, 0)), pl.BlockSpec((ne, 1), lambda bi, i: (0, 0))]
            out_shape += [jax.ShapeDtypeStruct((b, s, d // 2), jnp.uint32),
                          jax.ShapeDtypeStruct((ROUTE_LANES, b * s), F32)]
            out_specs += [pl.BlockSpec((None, ts, d // 2), lambda bi, i: (bi, i, 0)),
                          pl.BlockSpec((ROUTE_LANES, ts), lambda bi, i: (0, bi * per_s + i))]
    out = pl.pallas_call(
        functools.partial(_post_kernel, mode=mode),
        grid=(b, s // ts),
        in_specs=in_specs,
        out_specs=out_specs,
        out_shape=out_shape,
        compiler_params=_params("parallel", "parallel"),
        name="post",
    )(*ins)
    return out if nxt is not None else out[0]


def _cast_job_specs(jobs, grid_ij):
    gi, gj = grid_ij
    steps = gi * gj
    ins, in_specs, out_shapes, out_specs = [], [], [], []
    for arr, layer in jobs:
        _, r, c = arr.shape
        split = None
        for cb in (1, 2, 4, 8, 16, 32):
            rb = steps // cb
            if steps % cb == 0 and r % rb == 0 and (r // rb) % BF16_SUBLANES == 0 and c % (cb * LANES) == 0:
                split = (rb, cb)
                break
        assert split is not None, (arr.shape, grid_ij)
        rb, cb = split

        def in_map(i, j, *_, layer=layer, cb=cb):
            step = i * gj + j
            return (layer, step // cb, step % cb)

        def out_map(i, j, *_, cb=cb):
            step = i * gj + j
            return (step // cb, step % cb)

        ins.append(arr)
        in_specs.append(pl.BlockSpec((None, r // rb, c // cb), in_map))
        out_shapes.append(jax.ShapeDtypeStruct((r, c), BF16))
        out_specs.append(pl.BlockSpec((r // rb, c // cb), out_map))
    return ins, in_specs, out_shapes, out_specs


def _run_cast_jobs(src_refs, dst_refs):
    for src, dst in zip(src_refs, dst_refs):
        dst[...] = src[...].astype(dst.dtype)


def _mm_kernel(a_ref, w_ref, *rest, nk, sigmoid_from, njobs):
    o_ref = rest[njobs]
    scratch = rest[2 * njobs + 1:]
    _run_cast_jobs(rest[:njobs], rest[njobs + 1:2 * njobs + 1])
    def product(cols=slice(None)):
        return jnp.dot(a_ref[...], w_ref[:, cols], preferred_element_type=F32)

    if nk > 1:
        assert sigmoid_from is None
        (acc_ref,) = scratch
        k = pl.program_id(2)

        @pl.when(k == 0)
        def _():
            acc_ref[...] = product()

        @pl.when(jnp.logical_and(k > 0, k < nk - 1))
        def _():
            acc_ref[...] += product()

        @pl.when(k == nk - 1)
        def _():
            o_ref[...] = (acc_ref[...] + product()).astype(o_ref.dtype)
    elif sigmoid_from is None:
        o_ref[...] = product().astype(o_ref.dtype)
    else:
        j = pl.program_id(1)

        @pl.when(j < sigmoid_from)
        def _():
            o_ref[...] = product().astype(o_ref.dtype)

        @pl.when(j >= sigmoid_from)
        def _():
            half = o_ref.shape[1] // 2
            for cols in (slice(0, half), slice(half, 2 * half)):
                o_ref[:, cols] = _sigmoid(product(cols)).astype(o_ref.dtype)


def _mm(a, w, *, sigmoid_from_col=None, tm=1024, tn=1024, tk=4096, cast_jobs=()):
    m, kdim = a.shape
    n = w.shape[1]
    tm, tn, tk = _tile(m, tm), _tile(n, tn), _tile(kdim, tk)
    nk = kdim // tk
    sig = None
    if sigmoid_from_col is not None:
        assert sigmoid_from_col % tn == 0
        sig = sigmoid_from_col // tn
    grid = (m // tm, n // tn, nk)
    job_ins, job_in_specs, job_shapes, job_out_specs = _cast_job_specs(cast_jobs, grid[:2])
    out = pl.pallas_call(
        functools.partial(_mm_kernel, nk=nk, sigmoid_from=sig, njobs=len(cast_jobs)),
        grid=grid,
        in_specs=[
            pl.BlockSpec((tm, tk), lambda i, j, k: (i, k)),
            pl.BlockSpec((tk, tn), lambda i, j, k: (k, j)),
        ] + job_in_specs,
        out_specs=[pl.BlockSpec((tm, tn), lambda i, j, k: (i, j))] + job_out_specs,
        out_shape=[jax.ShapeDtypeStruct((m, n), BF16)] + job_shapes,
        scratch_shapes=[pltpu.VMEM((tm, tn), F32)] if nk > 1 else [],
        compiler_params=_params("parallel", "parallel", "arbitrary"),
        name="mm",
    )(a, w, *job_ins)
    return out[0], list(out[1:])


def _merge_kernel(y_ref, a_ref, wf_ref, wa_ref, gf_ref, ga_ref, o_ref):
    f = jnp.dot(y_ref[...], wf_ref[...], preferred_element_type=F32)
    a = jnp.dot(a_ref[...], wa_ref[...], preferred_element_type=F32)
    o = gf_ref[...].astype(F32) * f + ga_ref[...].astype(F32) * a
    o_ref[...] = o.astype(o_ref.dtype)


def _merge(y, a, p, w_f, w_a, gate_col0):
    m, f = y.shape
    q = a.shape[1]
    d = w_f.shape[1]
    tm, tn = _tile(m, 1024), _tile(d, 512)
    assert gate_col0 % tn == 0
    g0 = gate_col0 // tn
    nd = d // tn
    return pl.pallas_call(
        _merge_kernel,
        grid=(m // tm, nd),
        in_specs=[
            pl.BlockSpec((tm, f), lambda i, j: (i, 0)),
            pl.BlockSpec((tm, q), lambda i, j: (i, 0)),
            pl.BlockSpec((f, tn), lambda i, j: (0, j)),
            pl.BlockSpec((q, tn), lambda i, j: (0, j)),
            pl.BlockSpec((tm, tn), lambda i, j: (i, g0 + j)),
            pl.BlockSpec((tm, tn), lambda i, j: (i, g0 + nd + j)),
        ],
        out_specs=pl.BlockSpec((tm, tn), lambda i, j: (i, j)),
        out_shape=jax.ShapeDtypeStruct((m, d), BF16),
        compiler_params=_params("parallel", "parallel"),
        name="merge",
    )(y, a, w_f, w_a, p, p)


def _glu_kernel(h_ref, wg_ref, wu_ref, *rest, njobs):
    o_ref = rest[njobs]
    _run_cast_jobs(rest[:njobs], rest[njobs + 1:])
    h = h_ref[...]
    g = jnp.dot(h, wg_ref[...], preferred_element_type=F32)
    u = jnp.dot(h, wu_ref[...], preferred_element_type=F32)
    o_ref[...] = (g * _sigmoid(g) * u).astype(o_ref.dtype)


def _glu(h, w_gate, w_up, cast_jobs=()):
    m, d = h.shape
    f = w_gate.shape[1]
    tm, tn = _tile(m, 1024), _tile(f, 512)
    wspec = pl.BlockSpec((d, tn), lambda i, j: (0, j))
    grid = (m // tm, f // tn)
    job_ins, job_in_specs, job_shapes, job_out_specs = _cast_job_specs(cast_jobs, grid)
    out = pl.pallas_call(
        functools.partial(_glu_kernel, njobs=len(cast_jobs)),
        grid=grid,
        in_specs=[pl.BlockSpec((tm, d), lambda i, j: (i, 0)), wspec, wspec] + job_in_specs,
        out_specs=[pl.BlockSpec((tm, tn), lambda i, j: (i, j))] + job_out_specs,
        out_shape=[jax.ShapeDtypeStruct((m, f), BF16)] + job_shapes,
        compiler_params=_params("parallel", "parallel"),
        name="glu",
    )(h, w_gate, w_up, *job_ins)
    return out[0], list(out[1:])


EXPERT_TILE = 256


def _rank_kernel(route_ref, tri_ref, rank_ref, cnt_ref, carry, *, ne):
    @pl.when(pl.program_id(0) == 0)
    def _():
        carry[...] = jnp.zeros_like(carry)

    r = route_ref[...]
    lane = lax.broadcasted_iota(jnp.int32, (r.shape[0], 2 * ne), 1)
    sel = jnp.where(lane < ne, r[:, 2:3], r[:, 3:4] + float(ne))
    onehot = lane.astype(F32) == sel
    oh = jnp.where(onehot, 1.0, 0.0)
    before = jnp.dot(tri_ref[...], oh.astype(BF16), preferred_element_type=F32) + carry[...]
    ranks = jnp.where(onehot, before, 0.0)
    r0 = jnp.sum(jnp.where(lane < ne, ranks, 0.0), axis=-1, keepdims=True)
    r1 = jnp.sum(jnp.where(lane >= ne, ranks, 0.0), axis=-1, keepdims=True)
    out_lane = lax.broadcasted_iota(jnp.int32, rank_ref.shape, 1)
    rank_ref[...] = jnp.where(out_lane == 0, r0, jnp.where(out_lane == 1, r1, 0.0))
    carry[...] += jnp.sum(oh, axis=0, keepdims=True)
    cnt_ref[...] = carry[...]


def _rank(route, ne):
    m = route.shape[0]
    tr = _tile(m, 512)
    tri = np.tril(np.ones((tr, tr), np.float32), -1)
    return pl.pallas_call(
        functools.partial(_rank_kernel, ne=ne),
        grid=(m // tr,),
        in_specs=[pl.BlockSpec((tr, ROUTE_LANES), lambda i: (i, 0)),
                  pl.BlockSpec((tr, tr), lambda i: (0, 0))],
        out_specs=[pl.BlockSpec((tr, ROUTE_LANES), lambda i: (i, 0)),
                   pl.BlockSpec((1, 2 * ne), lambda i: (0, 0))],
        out_shape=[jax.ShapeDtypeStruct((m, ROUTE_LANES), F32),
                   jax.ShapeDtypeStruct((1, 2 * ne), F32)],
        scratch_shapes=[pltpu.VMEM((1, 2 * ne), F32)],
        compiler_params=_params("arbitrary"),
        name="rank",
    )(route, jnp.asarray(tri).astype(BF16))


def _moe_plan(route, rank, counts, ne, tm):
    m = route.shape[0]
    nt = (2 * m) // tm + ne
    c0 = counts[0, :ne].astype(jnp.int32)
    c = c0 + counts[0, ne:].astype(jnp.int32)
    padded = ((c + tm - 1) // tm) * tm
    ends = jnp.cumsum(padded)
    off = ends - padded
    i1 = route[:, 2].astype(jnp.int32)
    i2 = route[:, 3].astype(jnp.int32)
    pos0 = off[i1] + rank[:, 0].astype(jnp.int32)
    pos1 = off[i2] + c0[i2] + rank[:, 1].astype(jnp.int32)
    tok = jnp.arange(m, dtype=jnp.int32)
    pos = jnp.concatenate([pos0, pos1])
    src = jnp.zeros((nt * tm,), jnp.int32).at[pos].set(jnp.concatenate([tok, tok]), unique_indices=True)
    tile_expert = jnp.sum(jnp.arange(nt, dtype=jnp.int32)[:, None] * tm >= ends[None, :], axis=1)
    tile_expert = jnp.minimum(tile_expert, ne - 1).astype(jnp.int32)
    n_used = (ends[-1] // tm).astype(jnp.int32).reshape(1)
    return dict(src=src, pos=pos, tile_expert=tile_expert, n_used=n_used, nt=nt)


GATHER_DEPTH = 3


def _glu_sparse_kernel(src_ref, te_ref, nu_ref, h_hbm, wg_ref, wu_ref, o_ref, *scratch, tm):
    bufs, sem = scratch[:GATHER_DEPTH], scratch[GATHER_DEPTH]
    ahead = GATHER_DEPTH - 1
    t = pl.program_id(0)
    nt = pl.num_programs(0)
    n_used = nu_ref[0]

    def row_copy(row, sl, r):
        return pltpu.make_async_copy(h_hbm.at[pl.ds(row, 1)], bufs[sl].at[pl.ds(r, 1)], sem.at[sl])

    def wait_tile(sl):
        def body(r, carry):
            row_copy(0, sl, 0).wait()
            return carry
        lax.fori_loop(0, tm, body, 0, unroll=8)

    @pl.when(t == 0)
    def _():
        for sl in range(ahead):
            def body(r, carry, sl=sl):
                row_copy(src_ref[sl * tm + r], sl, r).start()
                return carry
            lax.fori_loop(0, tm, body, 0, unroll=8)

    def used_step(sl):
        wait_tile(sl)
        nxt = jnp.minimum(t + ahead, nt - 1) * tm
        dst = (sl + ahead) % GATHER_DEPTH
        x = _unpack_halves(bufs[sl][...]).astype(BF16)
        for r in range(tm):
            row_copy(src_ref[nxt + r], dst, r).start(priority=r % 2)
        g = jnp.dot(x, wg_ref[...], preferred_element_type=F32)
        u = jnp.dot(x, wu_ref[...], preferred_element_type=F32)
        o_ref[...] = (g * _sigmoid(g) * u).astype(o_ref.dtype)

    for sl in range(GATHER_DEPTH):
        @pl.when(jnp.logical_and(t < n_used, t % GATHER_DEPTH == sl))
        def _(sl=sl):
            used_step(sl)

        @pl.when(jnp.logical_and(t == n_used - 1, t % GATHER_DEPTH == sl))
        def _(sl=sl):
            for k in range(1, GATHER_DEPTH):
                wait_tile((sl + k) % GATHER_DEPTH)

    @pl.when(t >= n_used)
    def _():
        o_ref[...] = jnp.zeros_like(o_ref)


def _glu_sparse(h, w_gate, w_up, plan, tm):
    _, d, fe = w_gate.shape
    nt = plan["nt"]
    wspec = pl.BlockSpec((None, d, fe), lambda t, src, te, nu: (te[t], 0, 0))
    return pl.pallas_call(
        functools.partial(_glu_sparse_kernel, tm=tm),
        grid_spec=pltpu.PrefetchScalarGridSpec(
            num_scalar_prefetch=3,
            grid=(nt,),
            in_specs=[pl.BlockSpec(memory_space=pl.ANY), wspec, wspec],
            out_specs=pl.BlockSpec((tm, fe), lambda t, src, te, nu: (t, 0)),
            scratch_shapes=[pltpu.VMEM((tm, d // 2), jnp.uint32)] * GATHER_DEPTH
            + [pltpu.SemaphoreType.DMA((GATHER_DEPTH,))],
        ),
        out_shape=jax.ShapeDtypeStruct((nt * tm, fe), BF16),
        compiler_params=_params("arbitrary"),
        name="glu_sparse",
    )(plan["src"], plan["tile_expert"], plan["n_used"], h, w_gate, w_up)


def _down_sparse_kernel(te_ref, nu_ref, hid_ref, w_ref, o_ref):
    t = pl.program_id(0)

    @pl.when(t < nu_ref[0])
    def _():
        o_ref[...] = _pack_halves(jnp.dot(hid_ref[...], w_ref[...], preferred_element_type=F32))

    @pl.when(t >= nu_ref[0])
    def _():
        o_ref[...] = jnp.zeros_like(o_ref)


def _down_sparse(hid, w_down, plan, tm):
    _, fe, d = w_down.shape
    nt = plan["nt"]
    return pl.pallas_call(
        _down_sparse_kernel,
        grid_spec=pltpu.PrefetchScalarGridSpec(
            num_scalar_prefetch=2,
            grid=(nt,),
            in_specs=[pl.BlockSpec((tm, fe), lambda t, te, nu: (t, 0)),
                      pl.BlockSpec((None, fe, d), lambda t, te, nu: (te[t], 0, 0))],
            out_specs=pl.BlockSpec((tm, d // 2), lambda t, te, nu: (t, 0)),
        ),
        out_shape=jax.ShapeDtypeStruct((nt * tm, d // 2), jnp.uint32),
        compiler_params=_params("arbitrary"),
        name="down_sparse",
    )(plan["tile_expert"], plan["n_used"], hid, w_down)


def _post_moe_kernel(pos_ref, x_ref, route_ref, gp_ref, ga_ref, ys_hbm, *rest, ts, m_total, with_next):
    if with_next:
        gn_ref, sh_ref, sc_ref, xo_ref, h_ref, buf0, buf1, sem = rest
    else:
        xo_ref, buf0, buf1, sem = rest
    i = pl.program_id(0)
    last = pl.num_programs(0) - 1
    bufs = (buf0, buf1)

    def row_copy(row, sl, k, r):
        return pltpu.make_async_copy(ys_hbm.at[pl.ds(row, 1)], bufs[sl].at[k, pl.ds(r, 1)], sem.at[sl])

    def wait_tile(sl):
        def body(r, carry):
            row_copy(0, sl, 0, 0).wait()
            return carry
        lax.fori_loop(0, 2 * ts, body, 0, unroll=8)

    @pl.when(i == 0)
    def _():
        def body(r, carry):
            for k in range(2):
                row_copy(pos_ref[k * m_total + r], 0, k, r).start()
            return carry
        lax.fori_loop(0, ts, body, 0, unroll=4)

    def step(sl):
        wait_tile(sl)
        nxt = jnp.minimum(i + 1, last) * ts
        route = route_ref[...]
        y = (route[:, 0:1] * _unpack_halves(bufs[sl][0]) + route[:, 1:2] * _unpack_halves(bufs[sl][1]))
        for r in range(ts):
            for k in range(2):
                row_copy(pos_ref[k * m_total + nxt + r], 1 - sl, k, r).start()
        xn = x_ref[...] + ga_ref[...] * _rms(y, gp_ref[...])
        xo_ref[...] = xn
        if with_next:
            h = _rms(xn, gn_ref[...]) * (1.0 + sc_ref[...]) + sh_ref[...]
            h_ref[...] = h.astype(h_ref.dtype)

    for sl in range(2):
        @pl.when(i % 2 == sl)
        def _(sl=sl):
            step(sl)

        @pl.when(jnp.logical_and(i == last, i % 2 == sl))
        def _(sl=sl):
            wait_tile(1 - sl)


def _post_moe(x, ys, route, plan, g_post, gate, nxt=None):
    b, s, d = x.shape
    m = b * s
    ts = _tile(s, 256)
    per_s = s // ts
    tok = pl.BlockSpec((ts, d), lambda i, pos: (i, 0))
    vec = pl.BlockSpec((1, d), lambda i, pos: (0, 0))
    per_seq = pl.BlockSpec((None, 1, d), lambda i, pos: (i // per_s, 0, 0))
    ins = [x.reshape(m, d), route, g_post.reshape(1, d), gate, ys]
    in_specs = [tok, pl.BlockSpec((ts, ROUTE_LANES), lambda i, pos: (i, 0)), vec, per_seq,
                pl.BlockSpec(memory_space=pl.ANY)]
    out_shape = [jax.ShapeDtypeStruct((m, d), F32)]
    out_specs = [tok]
    if nxt is not None:
        g_next, shift, scale = nxt
        ins += [g_next.reshape(1, d), shift, scale]
        in_specs += [vec, per_seq, per_seq]
        out_shape.append(jax.ShapeDtypeStruct((m, d), BF16))
        out_specs.append(tok)
    out = pl.pallas_call(
        functools.partial(_post_moe_kernel, ts=ts, m_total=m, with_next=nxt is not None),
        grid_spec=pltpu.PrefetchScalarGridSpec(
            num_scalar_prefetch=1,
            grid=(m // ts,),
            in_specs=in_specs,
            out_specs=out_specs,
            scratch_shapes=[pltpu.VMEM((2, ts, d // 2), jnp.uint32), pltpu.VMEM((2, ts, d // 2), jnp.uint32),
                            pltpu.SemaphoreType.DMA((2,))],
        ),
        out_shape=out_shape,
        compiler_params=_params("arbitrary"),
        name="post_moe",
    )(plan["pos"], *ins)
    if nxt is not None:
        return out[0].reshape(b, s, d), out[1].reshape(b, s, d)
    return out[0].reshape(b, s, d)


LOG2E = 1.4426950408889634
ATTN_HEADS_PER_PASS = 4


def _attn_kernel(q_ref, kp_ref, km_ref, kn_ref, vp_ref, vm_ref, vn_ref, bias_ref, sink_ref, o_ref,
                 k_scr, v_scr, *, tq, seq, group, hd, scale):
    w = WINDOW
    i = pl.program_id(2)
    k_scr[0:w] = kp_ref[...]
    k_scr[w:w + tq] = km_ref[...]
    k_scr[w + tq:] = kn_ref[...]
    v_scr[0:w, 0:hd] = vp_ref[...]
    v_scr[w:w + tq, 0:hd] = vm_ref[...]
    v_scr[w + tq:, 0:hd] = vn_ref[...]
    v_scr[:, hd:] = jnp.ones((tq + 2 * w, hd), v_scr.dtype)
    c = scale * LOG2E
    bias = bias_ref[...]
    sink_rep = sink_ref[...]
    sink = sink_rep[:, 0:1]
    nsub = tq // w
    hpp = ATTN_HEADS_PER_PASS
    for s in range(nsub):
        kw = k_scr[s * w:(s + 3) * w]
        vw = v_scr[s * w:(s + 3) * w]
        pen = None
        if s == 0 or s == nsub - 1:
            key_pos = lax.broadcasted_iota(jnp.int32, (1, 3 * w), 1) + (i * tq + (s - 1) * w)
            valid = jnp.logical_and(key_pos >= 0, key_pos < seq)
            pen = jnp.where(valid, 0.0, NEG_BIG)
        for h0 in range(0, group, hpp):
            rows = slice(h0 * w, (h0 + hpp) * w)
            q = jnp.concatenate([q_ref[s * w:(s + 1) * w, h * hd:(h + 1) * hd] for h in range(h0, h0 + hpp)],
                                axis=0)
            sc = lax.dot_general(q, kw, (((1,), (1,)), ((), ())), preferred_element_type=F32) + bias[rows]
            if pen is not None:
                sc = sc + pen
            mx = jnp.maximum(jnp.max(sc, axis=-1, keepdims=True), sink[rows])
            p = jnp.exp2((sc - mx) * c)
            pv = jnp.dot(p.astype(BF16), vw, preferred_element_type=F32)
            denom = pv[:, hd:] + jnp.exp2((sink_rep[rows] - mx) * c)
            o = (pv[:, 0:hd] / denom).astype(o_ref.dtype)
            for h in range(hpp):
                o_ref[s * w:(s + 1) * w, (h0 + h) * hd:(h0 + h + 1) * hd] = o[h * w:(h + 1) * w]


def _attention(p, sink, q_col0, k_col0, v_col0):
    b, s, _ = p.shape
    w = WINDOW
    group = N_HEADS // N_KV_HEADS
    hd = LANES
    gw = group * hd
    tq = _tile(s, 1024)
    assert s % w == 0 and tq % w == 0 and q_col0 % gw == 0 and k_col0 % hd == 0 and v_col0 % hd == 0
    r = tq // w
    nblk = s // w
    qb, kb, vb = q_col0 // gw, k_col0 // hd, v_col0 // hd
    scale = float(hd) ** -0.5

    def halo_specs(c0):
        return [
            pl.BlockSpec((None, w, hd), lambda bi, g, i: (bi, jnp.maximum(i * r - 1, 0), c0 + g)),
            pl.BlockSpec((None, tq, hd), lambda bi, g, i: (bi, i, c0 + g)),
            pl.BlockSpec((None, w, hd), lambda bi, g, i: (bi, jnp.minimum((i + 1) * r, nblk - 1), c0 + g)),
        ]

    rel = np.abs(np.arange(w)[:, None] + w - np.arange(3 * w)[None, :]).astype(np.float64)
    slopes = np.exp2(-8.0 * np.arange(1, N_HEADS + 1, dtype=np.float64) / N_HEADS)
    bias = np.where(rel[None] <= WINDOW, -slopes[:, None, None] * rel[None] / scale, NEG_BIG)
    bias = bias.reshape(N_HEADS * w, 3 * w).astype(np.float32)
    sink_rows = jnp.broadcast_to((sink.astype(F32) / scale)[:, None, None], (N_HEADS, w, LANES))
    sink_rows = sink_rows.reshape(N_HEADS * w, LANES)
    return pl.pallas_call(
        functools.partial(_attn_kernel, tq=tq, seq=s, group=group, hd=hd, scale=scale),
        grid=(b, N_KV_HEADS, s // tq),
        in_specs=[pl.BlockSpec((None, tq, gw), lambda bi, g, i: (bi, i, qb + g))]
        + halo_specs(kb) + halo_specs(vb)
        + [pl.BlockSpec((group * w, 3 * w), lambda bi, g, i: (g, 0)),
           pl.BlockSpec((group * w, LANES), lambda bi, g, i: (g, 0))],
        out_specs=pl.BlockSpec((None, tq, gw), lambda bi, g, i: (bi, i, g)),
        out_shape=jax.ShapeDtypeStruct((b, s, N_HEADS * hd), BF16),
        scratch_shapes=[pltpu.VMEM((tq + 2 * w, hd), BF16), pltpu.VMEM((tq + 2 * w, 2 * hd), BF16)],
        compiler_params=_params("parallel", "parallel", "parallel"),
        name="attention",
    )(p, p, p, p, p, p, p, jnp.asarray(bias), sink_rows)


R_FFT = BF16_SUBLANES


def _fft_factors(s):
    n2 = min(256, s // R_FFT)
    n1 = s // n2
    assert n1 * n2 == s and n1 % R_FFT == 0 and n2 % R_FFT == 0
    return n1, n2


def _fft_tables(s, gdim):
    n1, n2 = _fft_factors(s)
    r = R_FFT
    k1 = np.arange(n1).reshape(1, n1, 1, 1, 1, 1)
    jj = np.arange(r).reshape(1, 1, 1, r, 1, 1)
    nn1 = np.arange(n1).reshape(1, 1, 1, 1, n1, 1)
    jc = np.arange(r).reshape(1, 1, 1, 1, 1, r)
    jb = np.arange(n2 // r).reshape(n2 // r, 1, 1, 1, 1, 1)
    ang = 2.0 * np.pi * ((nn1 * k1 * n2 + (jb * r + jj) * k1) % s) / s
    ang = np.broadcast_to(ang, (n2 // r, n1, 1, r, n1, 1))
    eye = (jj == jc)
    t1 = np.concatenate([np.cos(ang) * eye, np.sin(ang) * eye], axis=2)
    t1 = t1.reshape(n2 // r, n1 * 2 * r, n1 * r)
    cc = np.arange(gdim)
    phi = 2.0 * np.pi * ((cc[:, None] * cc[None, :]) % gdim) / gdim
    wc = np.block([[np.cos(phi), np.sin(phi)], [-np.sin(phi), np.cos(phi)]]) / np.sqrt(gdim)
    k2 = np.arange(n2).reshape(n2, 1, 1)
    nn2 = (np.arange(n2 // r).reshape(1, n2 // r, 1) * r + np.arange(r).reshape(1, 1, r))
    beta = 2.0 * np.pi * ((k2 * nn2) % n2) / n2
    f2 = np.stack([np.cos(beta), -np.sin(beta)], axis=2).reshape(n2, 2 * n2) / np.sqrt(s)
    return (jnp.asarray(t1.astype(np.float32)).astype(BF16),
            jnp.asarray(wc.astype(np.float32)).astype(BF16),
            jnp.asarray(f2.astype(np.float32)).astype(BF16))


def _fft1_kernel(t_ref, wc_ref, u_ref, z_ref, *, gdim):
    n1, r, c = u_ref.shape
    u = u_ref[...].reshape(n1 * r, c)
    g = jnp.dot(t_ref[...], u, preferred_element_type=F32).astype(BF16).reshape(n1, 2, r, c)
    gr = g[:, 0].reshape(n1 * r, c)
    gi = g[:, 1].reshape(n1 * r, c)
    ps, qs = [], []
    for grp in range(c // gdim):
        cols = slice(grp * gdim, (grp + 1) * gdim)
        pq = jnp.dot(jnp.concatenate([gr[:, cols], gi[:, cols]], axis=1), wc_ref[...], preferred_element_type=F32)
        ps.append(pq[:, :gdim])
        qs.append(pq[:, gdim:])
    z_ref[:, 0] = jnp.concatenate(ps, axis=1).reshape(n1, r, c).astype(z_ref.dtype)
    z_ref[:, 1] = jnp.concatenate(qs, axis=1).reshape(n1, r, c).astype(z_ref.dtype)


def _fft2_kernel(f2_ref, z_ref, y_ref, scr):
    nj, rows, ct = z_ref.shape
    per_k1 = rows // R_FFT
    n2 = nj * R_FFT
    for kk in range(R_FFT):
        zin = z_ref[:, kk * per_k1:(kk + 1) * per_k1, :].reshape(2 * n2, ct)
        scr[:, kk, :] = jnp.dot(f2_ref[...], zin, preferred_element_type=F32)
    y_ref[...] = scr[...].astype(y_ref.dtype)


def _fourier(p, fwidth):
    b, s, width = p.shape
    gdim = fwidth // F_GROUPS
    assert gdim == LANES
    n1, n2 = _fft_factors(s)
    r = R_FFT
    t1, wc, f2 = _fft_tables(s, gdim)
    nj = n2 // r
    z = pl.pallas_call(
        functools.partial(_fft1_kernel, gdim=gdim),
        grid=(nj, b),
        in_specs=[
            pl.BlockSpec((None, 2 * n1 * r, n1 * r), lambda j, bi: (j, 0, 0)),
            pl.BlockSpec((2 * gdim, 2 * gdim), lambda j, bi: (0, 0)),
            pl.BlockSpec((None, n1, r, fwidth), lambda j, bi: (bi, 0, j, 0)),
        ],
        out_specs=pl.BlockSpec((None, None, n1, 2, r, fwidth), lambda j, bi: (bi, j, 0, 0, 0, 0)),
        out_shape=jax.ShapeDtypeStruct((b, nj, n1, 2, r, fwidth), BF16),
        compiler_params=_params("parallel", "parallel"),
        name="fft_stage1",
    )(t1, wc, p.reshape(b, n1, n2, width))
    ct = _tile(fwidth, 512)
    nkb = n1 // r
    z = z.reshape(b, nj, nkb, 2 * r * r, fwidth)
    y = pl.pallas_call(
        _fft2_kernel,
        grid=(b, nkb, fwidth // ct),
        in_specs=[
            pl.BlockSpec((n2, 2 * n2), lambda bi, kb, c: (0, 0)),
            pl.BlockSpec((None, nj, None, 2 * r * r, ct), lambda bi, kb, c: (bi, 0, kb, 0, c)),
        ],
        out_specs=pl.BlockSpec((None, n2, None, r, ct), lambda bi, kb, c: (bi, 0, kb, 0, c)),
        out_shape=jax.ShapeDtypeStruct((b, n2, nkb, r, fwidth), BF16),
        scratch_shapes=[pltpu.VMEM((n2, r, ct), F32)],
        compiler_params=_params("parallel", "parallel", "parallel"),
        name="fft_stage2",
    )(f2, z)
    return y.reshape(b, s, fwidth)


class _Bf16Weights:
    def __init__(self, stacked):
        self.stacked = stacked
        self.ready = {}

    def get(self, name, layer):
        key = (name, layer)
        if key not in self.ready:
            self.ready[key] = self.stacked[name][layer].astype(BF16)
        return self.ready[key]

    def jobs(self, keys):
        todo = [k for k in keys if k not in self.ready]
        return todo, [(self.stacked[name], layer) for name, layer in todo]

    def done(self, todo, arrays):
        self.ready.update(zip(todo, arrays))


def _trunk(x, mod, wts, bfw):
    b, s, d = x.shape
    m = b * s
    depth = wts["g_pre_mix"].shape[0]
    fwidth = d // 2
    qwidth = d
    kvwidth = N_KV_HEADS * (d // N_HEADS)
    q0 = fwidth
    k0 = q0 + qwidth
    v0 = k0 + kvwidth
    g0 = v0 + kvwidth
    ne, fe = wts["n_experts"], wts["expert_width"]

    def mods(l):
        return [mod[l, :, k * d:(k + 1) * d].reshape(b, 1, d) for k in range(N_MOD)]

    def branch_keys(l):
        return [("w_fourier_out", l), ("w_attn_out", l), ("w_out", l)]

    sh1, sc1, ga1, sh2, sc2, ga2 = mods(0)
    h = _prenorm(x, wts["g_pre_mix"][0], sh1, sc1)
    for l in range(depth):
        i = l // 2
        dense = l % 2 == 0
        ffn_keys = ([("w_dense_gate", i), ("w_dense_up", i)] if dense
                    else [("w_moe_gate", i), ("w_moe_up", i), ("w_moe_down", i)])
        todo, jobs = bfw.jobs(branch_keys(l) + ffn_keys)
        p, cast = _mm(h.reshape(m, d), bfw.get("w_in", l), sigmoid_from_col=g0, cast_jobs=jobs)
        bfw.done(todo, cast)
        p3 = p.reshape(b, s, -1)
        y = _fourier(p3, fwidth)
        a = _attention(p3, wts["attn_sink"][l], q0, k0, v0)
        mg = _merge(y.reshape(m, fwidth), a.reshape(m, qwidth), p, bfw.get("w_fourier_out", l),
                    bfw.get("w_attn_out", l), g0)
        more = l + 1 < depth
        todo, jobs = bfw.jobs(branch_keys(l + 1) if more else [])
        mo, cast = _mm(mg, bfw.get("w_out", l), cast_jobs=jobs)
        bfw.done(todo, cast)
        router = None if dense else (wts["w_router"][i], wts["b_router"][i])
        x, h2, *route_t = _post(x, mo, wts["g_post_mix"][l], ga1, (wts["g_pre_ffn"][l], sh2, sc2), router=router)
        gate2 = ga2
        nxt = None
        if more:
            sh1, sc1, ga1, sh2, sc2, ga2 = mods(l + 1)
            nxt = (wts["g_pre_mix"][l + 1], sh1, sc1)
        if dense:
            todo, jobs = bfw.jobs([("w_dense_down", i)] + ([("w_in", l + 1)] if more else []))
            hid, cast = _glu(h2.reshape(m, d), bfw.get("w_dense_gate", i), bfw.get("w_dense_up", i), cast_jobs=jobs)
            bfw.done(todo, cast)
            yo, _ = _mm(hid, bfw.get("w_dense_down", i))
            out = _post(x, yo, wts["g_post_ffn"][l], gate2, nxt)
        else:
            route = route_t[0].T
            rank, counts = _rank(route, ne)
            plan = _moe_plan(route, rank, counts, ne, EXPERT_TILE)
            hid = _glu_sparse(h2.reshape(m, d // 2), bfw.get("w_moe_gate", i).reshape(ne, d, fe),
                              bfw.get("w_moe_up", i).reshape(ne, d, fe), plan, EXPERT_TILE)
            ys = _down_sparse(hid, bfw.get("w_moe_down", i).reshape(ne, fe, d), plan, EXPERT_TILE)
            out = _post_moe(x, ys, route, plan, wts["g_post_ffn"][l], gate2, nxt)
        x, h = out if nxt is not None else (out, None)
    return x


def kernel(x_prompt, x_sample, c_prompt, c_sample, w_ada, b_ada, g_pre_mix, g_post_mix, w_in, w_fourier_out,
           w_attn_out, attn_sink, w_out, g_pre_ffn, g_post_ffn, w_dense_gate, w_dense_up, w_dense_down,
           w_router, b_router, w_moe_gate, w_moe_up, w_moe_down):
    bp, bs = c_prompt.shape[0], c_sample.shape[0]
    d = c_prompt.shape[1]
    rows = -(-(bp + bs) // 8) * 8
    c_all = jnp.zeros((rows, d), F32).at[:bp].set(c_prompt).at[bp:bp + bs].set(c_sample)
    mod = _adaln(c_all, w_ada, b_ada)

    n_moe, ne, _, fe = w_moe_gate.shape
    bfw = _Bf16Weights(dict(
        w_in=w_in, w_fourier_out=w_fourier_out, w_attn_out=w_attn_out, w_out=w_out,
        w_dense_gate=w_dense_gate, w_dense_up=w_dense_up, w_dense_down=w_dense_down,
        w_moe_gate=w_moe_gate.reshape(n_moe, ne * d, fe), w_moe_up=w_moe_up.reshape(n_moe, ne * d, fe),
        w_moe_down=w_moe_down.reshape(n_moe, ne * fe, d),
    ))
    wts = dict(
        w_router=w_router, b_router=b_router, attn_sink=attn_sink, n_experts=ne, expert_width=fe,
        g_pre_mix=g_pre_mix, g_post_mix=g_post_mix, g_pre_ffn=g_pre_ffn, g_post_ffn=g_post_ffn,
    )
    y_prompt = _trunk(x_prompt, mod[:, :bp], wts, bfw)
    y_sample = _trunk(x_sample, mod[:, bp:bp + bs], wts, bfw)
    return (y_prompt, y_sample)
```
